```python
import jax
import jax.numpy as jnp
from jax import lax
import numpy as np

D_MODEL = 2048
BATCH = 32
SEQ = 256
DEPTH = 4
DEC_BATCH = 2
DEC_SEQ = 1024
PAST_LEN = 512

GRID_W = 64
N_MIXERS = 4
RMS_EPS = 1e-6
N_MOD = 6

D_RNN = D_MODEL
LRU_BLOCKS = 8
LRU_BLOCK_W = D_RNN // LRU_BLOCKS
CONV_W = 4
LRU_C = 8.0

POOL_WINDOWS = (2, 4, 8, 16)
POOL_GROUP_W = D_MODEL // len(POOL_WINDOWS)

RWKV_HEAD = 64
RWKV_HEADS = D_MODEL // RWKV_HEAD
DECAY_LORA = 96
ICLR_LORA = 96
GATE_LORA = 256
RWKV_GN_EPS = 64e-5
N_TOKEN_MIX = 6

N_HEADS = 32
N_KV_HEADS = 4
KV_GROUP = N_HEADS // N_KV_HEADS
HEAD_DIM = 64
WINDOW = 128
ATTN_BLOCK = WINDOW
ROPE_BASE = 10000.0
ATTN_SCALE = HEAD_DIM ** -0.5
NEG_INF = -1e30

N_EXPERTS = 32
TOP_K = 4
D_EXPERT = D_MODEL
SWIGLU_LIMIT = 7.0
SWIGLU_ALPHA = 1.702
MOE_BLOCK = 256

kernel_name = 'hybrid_prefix_diffusion_trunk_step'


def _n_layers_of(kind):
    return len(range(kind, DEPTH, N_MIXERS))


def _rmsnorm(x, g):
    xf = x.astype(jnp.float32)
    y = xf * lax.rsqrt(jnp.mean(jnp.square(xf), axis=-1, keepdims=True) + RMS_EPS)
    return (y * g.astype(jnp.float32)).astype(x.dtype)


def _modulate(h, shift, scale):
    return h * (1 + scale) + shift


def _conv_centred(x, w, b):
    left = CONV_W // 2
    y = lax.conv_general_dilated(x, w[:, None, :].astype(x.dtype), window_strides=(1,),
                                 padding=[(left, CONV_W - 1 - left)],
                                 dimension_numbers=('NWC', 'WIO', 'NWC'),
                                 feature_group_count=x.shape[-1])
    return y + b


def _rglru_direction(xc, w_a, b_a, w_x, b_x, lam, h0, reverse):
    xb = xc.reshape(xc.shape[:-1] + (LRU_BLOCKS, LRU_BLOCK_W))
    r = jax.nn.sigmoid(jnp.einsum('bsnc,ncd->bsnd', xb, w_a).reshape(xc.shape) + b_a)
    i = jax.nn.sigmoid(jnp.einsum('bsnc,ncd->bsnd', xb, w_x).reshape(xc.shape) + b_x)
    log_a = -LRU_C * r.astype(jnp.float32) * jax.nn.softplus(-lam.astype(jnp.float32))
    a = jnp.exp(log_a)
    u = jnp.sqrt(-jnp.expm1(2.0 * log_a)) * (i * xc).astype(jnp.float32)

    def step(h, au):
        h = au[0] * h + au[1]
        return h, h

    h_last, hs = lax.scan(step, h0.astype(jnp.float32),
                          (jnp.swapaxes(a, 0, 1), jnp.swapaxes(u, 0, 1)), reverse=reverse)
    return jnp.swapaxes(hs, 0, 1), h_last


def _rglru_mixer(h, h0_f, h0_b, w_in, conv_w, conv_b, w_a, b_a, w_x, b_x, lam, w_out):
    gate_in, x_in = jnp.split(h @ w_in, 2, axis=-1)
    xc = _conv_centred(x_in, conv_w, conv_b)
    ys_f, hf = _rglru_direction(xc, w_a[0], b_a[0], w_x[0], b_x[0], lam[0], h0_f, False)
    ys_b, hb = _rglru_direction(xc, w_a[1], b_a[1], w_x[1], b_x[1], lam[1], h0_b, True)
    y = (ys_f + ys_b).astype(h.dtype) * jax.nn.gelu(gate_in)
    return y @ w_out, hf, hb


def _pool_mixer(h, w, b, scale):
    bsz, seq, _ = h.shape
    hf = h.astype(jnp.float32)
    pad = max(POOL_WINDOWS) // 2
    cs = jnp.cumsum(jnp.pad(hf, ((0, 0), (pad + 1, pad), (0, 0))), axis=1)
    t = jnp.arange(seq)
    pooled = []
    for g, win in enumerate(POOL_WINDOWS):
        left = win // 2
        right = win - 1 - left
        ch = slice(g * POOL_GROUP_W, (g + 1) * POOL_GROUP_W)
        total = (cs[:, pad + right + 1:pad + right + 1 + seq, ch]
                 - cs[:, pad - left:pad - left + seq, ch])
        count = (jnp.minimum(t + right, seq - 1) - jnp.maximum(t - left, 0) + 1).astype(jnp.float32)
        pooled.append(total / count[None, :, None])
    dpool = (jnp.concatenate(pooled, axis=-1) - hf).astype(h.dtype)
    dg = dpool.reshape(bsz, seq, len(POOL_WINDOWS), POOL_GROUP_W)
    y = jnp.einsum('bsgc,gcd->bsgd', dg, w) + b
    return y.reshape(bsz, seq, D_MODEL) * scale


def _rwkv_scan(r, w, k, v, a, b, s0, reverse):
    def step(s, inp):
        r_t, w_t, k_t, v_t, a_t, b_t = inp
        sa = jnp.einsum('bhvk,bhk->bhv', s, a_t)
        s = (s * w_t[:, :, None, :] + sa[..., None] * b_t[:, :, None, :]
             + v_t[..., None] * k_t[:, :, None, :])
        return s, jnp.einsum('bhvk,bhk->bhv', s, r_t)

    xs = tuple(jnp.swapaxes(z, 0, 1) for z in (r, w, k, v, a, b))
    s_last, ys = lax.scan(step, s0.astype(jnp.float32), xs, reverse=reverse)
    return jnp.swapaxes(ys, 0, 1), s_last


def _rwkv_mixer(h, s0_f, s0_b, mix, w_r, w_k, w_v, w_o, dec_w0, dec_w1, dec_w2,
                a_w0, a_w1, a_w2, g_w1, g_w2, k_k, k_a, r_k, ln_w, ln_b):
    f32 = jnp.float32
    bsz, seq, _ = h.shape
    heads = (bsz, seq, RWKV_HEADS, RWKV_HEAD)
    hp = jnp.pad(h, ((0, 0), (1, 1), (0, 0)))
    xx = 0.5 * (hp[:, :-2] + hp[:, 2:]) - h
    xr, xw, xk, xv, xa, xg = (h + xx * mix[n] for n in range(N_TOKEN_MIX))
    r = (xr @ w_r).astype(f32).reshape(heads)
    k = (xk @ w_k).astype(f32).reshape(heads)
    v = (xv @ w_v).astype(f32).reshape(heads)
    g = jax.nn.sigmoid(xg @ g_w1) @ g_w2
    kk = k * k_k.astype(f32).reshape(RWKV_HEADS, RWKV_HEAD)
    kk = kk / jnp.maximum(jnp.sqrt(jnp.sum(kk * kk, axis=-1, keepdims=True)), 1e-12)
    k_a_h = k_a.astype(f32).reshape(RWKV_HEADS, RWKV_HEAD)
    outs, states = [], []
    for d, s0 in enumerate((s0_f, s0_b)):
        w_log = -jax.nn.softplus(-(dec_w0[d] + jnp.tanh(xw @ dec_w1[d]) @ dec_w2[d]).astype(f32)) - 0.5
        decay = jnp.exp(-jnp.exp(w_log)).reshape(heads)
        iclr = jax.nn.sigmoid((a_w0[d] + (xa @ a_w1[d]) @ a_w2[d]).astype(f32)).reshape(heads)
        k_d = k * (1 + (iclr - 1) * k_a_h)
        y_d, s_d = _rwkv_scan(r, decay, k_d, v, -kk, kk * iclr, s0, d == 1)
        outs.append(y_d)
        states.append(s_d)
    y = outs[0] + outs[1]
    mu = jnp.mean(y, axis=-1, keepdims=True)
    var = jnp.mean(jnp.square(y - mu), axis=-1, keepdims=True)
    y = ((y - mu) * lax.rsqrt(var + RWKV_GN_EPS)).reshape(bsz, seq, D_MODEL)
    y = y * ln_w.astype(f32) + ln_b.astype(f32)
    bonus = jnp.sum(r * k * r_k.astype(f32), axis=-1, keepdims=True) * v
    y = (y + bonus.reshape(bsz, seq, D_MODEL)).astype(h.dtype)
    return (y * g) @ w_o, states[0], states[1]


def _qkv(h, w_qkv, b_qkv):
    bsz, seq, _ = h.shape
    qkv = h @ w_qkv + b_qkv
    nq = N_HEADS * HEAD_DIM
    nk = N_KV_HEADS * HEAD_DIM
    q = qkv[..., :nq].reshape(bsz, seq, N_KV_HEADS, KV_GROUP, HEAD_DIM)
    k = qkv[..., nq:nq + nk].reshape(bsz, seq, N_KV_HEADS, HEAD_DIM)
    v = qkv[..., nq + nk:].reshape(bsz, seq, N_KV_HEADS, HEAD_DIM)
    return q, k, v


def _sink_softmax(s, sink):
    m = jnp.maximum(jnp.max(s, axis=-1, keepdims=True), sink)
    p = jnp.exp(s - m)
    return p / (jnp.sum(p, axis=-1, keepdims=True) + jnp.exp(sink - m))


def _axial_rope_tables(seq):
    n_rows = seq // GRID_W
    rows = jnp.repeat(jnp.arange(n_rows), GRID_W).astype(jnp.float32)
    cols = jnp.tile(jnp.arange(GRID_W), n_rows).astype(jnp.float32)
    quarter = HEAD_DIM // 4
    inv_freq = ROPE_BASE ** (-jnp.arange(quarter, dtype=jnp.float32) / quarter)
    ang = jnp.stack([rows[:, None] * inv_freq, cols[:, None] * inv_freq], axis=1)
    return jnp.cos(ang), jnp.sin(ang)


def _apply_axial_rope(x, cos, sin):
    quarter = HEAD_DIM // 4
    shape = x.shape
    xr = x.astype(jnp.float32).reshape(shape[:-1] + (2, 2, quarter))
    bshape = (1, shape[1]) + (1,) * (x.ndim - 3) + (2, quarter)
    c = cos.reshape(bshape)
    s = sin.reshape(bshape)
    x1 = xr[..., 0, :]
    x2 = xr[..., 1, :]
    out = jnp.stack([x1 * c - x2 * s, x2 * c + x1 * s], axis=-2)
    return out.reshape(shape).astype(x.dtype)


def _attn_context(h, w_qkv, b_qkv, w_o, b_o, sinks):
    bsz, seq, _ = h.shape
    q, k, v = _qkv(h, w_qkv, b_qkv)
    sink = sinks.astype(jnp.float32).reshape(N_KV_HEADS, KV_GROUP, 1, 1)
    n_blk = seq // ATTN_BLOCK
    qb = jnp.moveaxis(q.reshape(bsz, n_blk, ATTN_BLOCK, N_KV_HEADS, KV_GROUP, HEAD_DIM), 1, 0)

    def one_block(q_blk):
        s = jnp.einsum('bqkgd,bmkd->bkgqm', q_blk, k, preferred_element_type=jnp.float32) * ATTN_SCALE
        p = _sink_softmax(s, sink)
        return jnp.einsum('bkgqm,bmkd->bqkgd', p.astype(v.dtype), v)

    o = jnp.moveaxis(lax.map(one_block, qb), 0, 1).reshape(bsz, seq, N_HEADS * HEAD_DIM)
    return o @ w_o + b_o, k, v


def _attn_latent(h, ck, cv, w_qkv, b_qkv, w_o, b_o, sinks):
    bsz, seq, _ = h.shape
    q, k, v = _qkv(h, w_qkv, b_qkv)
    cos, sin = _axial_rope_tables(seq)
    q = _apply_axial_rope(q, cos, sin)
    k = _apply_axial_rope(k, cos, sin)
    sink = sinks.astype(jnp.float32).reshape(N_KV_HEADS, KV_GROUP, 1, 1)
    n_blk = seq // ATTN_BLOCK
    qb = q.reshape(bsz, n_blk, ATTN_BLOCK, N_KV_HEADS, KV_GROUP, HEAD_DIM)

    def band(z):
        zp = jnp.pad(z, ((0, 0), (ATTN_BLOCK, ATTN_BLOCK), (0, 0), (0, 0)))
        zp = zp.reshape(bsz, n_blk + 2, ATTN_BLOCK, N_KV_HEADS, HEAD_DIM)
        return jnp.concatenate([zp[:, :-2], zp[:, 1:-1], zp[:, 2:]], axis=2)

    kb = band(k)
    vb = band(v)
    n_loc = 3 * ATTN_BLOCK
    q_pos = jnp.arange(seq).reshape(n_blk, ATTN_BLOCK)
    k_pos = (jnp.arange(n_blk)[:, None] - 1) * ATTN_BLOCK + jnp.arange(n_loc)[None, :]
    valid = ((k_pos[:, None, :] >= 0) & (k_pos[:, None, :] < seq)
             & (jnp.abs(q_pos[:, :, None] - k_pos[:, None, :]) <= WINDOW))
    s_loc = jnp.einsum('bnqkgd,bnmkd->bnkgqm', qb, kb, preferred_element_type=jnp.float32) * ATTN_SCALE
    s_loc = jnp.where(valid[None, :, None, None], s_loc, NEG_INF)
    s_ctx = jnp.einsum('bnqkgd,bmkd->bnkgqm', qb, ck, preferred_element_type=jnp.float32) * ATTN_SCALE
    p = _sink_softmax(jnp.concatenate([s_loc, s_ctx], axis=-1), sink).astype(v.dtype)
    o = (jnp.einsum('bnkgqm,bnmkd->bnqkgd', p[..., :n_loc], vb)
         + jnp.einsum('bnkgqm,bmkd->bnqkgd', p[..., n_loc:], cv))
    o = o.reshape(bsz, seq, N_HEADS * HEAD_DIM)
    return o @ w_o + b_o


def _moe(x, w_router, b_router, w_gu, b_gu, w_down, b_down):
    f32 = jnp.float32
    n_tok, d = x.shape
    n_assign = n_tok * TOP_K
    logits = jnp.dot(x, w_router, preferred_element_type=f32) + b_router.astype(f32)
    top_logit, top_e = lax.top_k(logits, TOP_K)
    gate = jax.nn.softmax(top_logit, axis=-1)
    flat_e = top_e.reshape(-1)
    order = jnp.argsort(flat_e)
    e_sorted = flat_e[order]
    tok_sorted = order // TOP_K
    gate_sorted = gate.reshape(-1)[order]
    counts = jnp.bincount(flat_e, length=N_EXPERTS)
    padded = (counts + MOE_BLOCK - 1) // MOE_BLOCK * MOE_BLOCK
    pad_end = jnp.cumsum(padded)
    pad_start = pad_end - padded
    grp_start = jnp.cumsum(counts) - counts
    dest = pad_start[e_sorted] + jnp.arange(n_assign) - grp_start[e_sorted]
    n_blocks = -(-n_assign // MOE_BLOCK) + N_EXPERTS
    rows = jnp.zeros((n_blocks * MOE_BLOCK, d), x.dtype).at[dest].set(x[tok_sorted])
    blk_e = jnp.minimum(jnp.searchsorted(pad_end, jnp.arange(n_blocks) * MOE_BLOCK, side='right'),
                        N_EXPERTS - 1)

    def expert_block(args):
        xb, e = args
        gu = xb @ w_gu[e] + b_gu[e]
        g = jnp.minimum(gu[:, :D_EXPERT], SWIGLU_LIMIT)
        u = jnp.clip(gu[:, D_EXPERT:], -SWIGLU_LIMIT, SWIGLU_LIMIT)
        return ((u + 1) * (g * jax.nn.sigmoid(SWIGLU_ALPHA * g))) @ w_down[e] + b_down[e]

    ys = lax.map(expert_block, (rows.reshape(n_blocks, MOE_BLOCK, d), blk_e))
    ys = ys.reshape(n_blocks * MOE_BLOCK, d)[dest].astype(f32) * gate_sorted[:, None]
    return jnp.zeros((n_tok, d), f32).at[tok_sorted].add(ys).astype(x.dtype)


def setup_inputs(seed: int = 0) -> dict:
    key = jax.random.key(seed)
    keys = iter(jax.random.split(key, 80))

    def nrm(shape, scale):
        return jax.random.normal(next(keys), shape, jnp.float32) * scale

    def gain(shape):
        return 1.0 + nrm(shape, 0.02)

    n_lru, n_pool, n_rwkv, n_attn = (_n_layers_of(m) for m in range(N_MIXERS))
    d = D_MODEL
    qkv_w = (N_HEADS + 2 * N_KV_HEADS) * HEAD_DIM
    p_lam = jax.random.uniform(next(keys), (n_lru, 2, D_RNN), jnp.float32, 0.9, 0.999)
    return {
        'x_prompt': nrm((BATCH, SEQ, d), 1.0),
        'x_sample': nrm((DEC_BATCH, DEC_SEQ, d), 1.0),
        'c': nrm((DEC_BATCH, d), 1.0),
        'c_ctx': nrm((d,), 1.0),
        'state_lru_fwd': nrm((DEC_BATCH, n_lru, D_RNN), 0.5),
        'state_lru_bwd': nrm((DEC_BATCH, n_lru, D_RNN), 0.5),
        'state_rwkv_fwd': nrm((DEC_BATCH, n_rwkv, RWKV_HEADS, RWKV_HEAD, RWKV_HEAD), 0.3),
        'state_rwkv_bwd': nrm((DEC_BATCH, n_rwkv, RWKV_HEADS, RWKV_HEAD, RWKV_HEAD), 0.3),
        'cache_attn_k': nrm((DEC_BATCH, n_attn, PAST_LEN, N_KV_HEADS, HEAD_DIM), 1.0),
        'cache_attn_v': nrm((DEC_BATCH, n_attn, PAST_LEN, N_KV_HEADS, HEAD_DIM), 1.0),
        'norm_mix': gain((DEPTH, d)),
        'norm_ffn': gain((DEPTH, d)),
        'w_mod': nrm((DEPTH, d, N_MOD * d), 0.5 * d ** -0.5),
        'b_mod': nrm((DEPTH, N_MOD * d), 0.02),
        'lru_w_in': nrm((n_lru, d, 2 * D_RNN), d ** -0.5),
        'lru_conv_w': nrm((n_lru, CONV_W, D_RNN), CONV_W ** -0.5),
        'lru_conv_b': nrm((n_lru, D_RNN), 0.02),
        'lru_w_a': nrm((n_lru, 2, LRU_BLOCKS, LRU_BLOCK_W, LRU_BLOCK_W), LRU_BLOCK_W ** -0.5),
        'lru_b_a': nrm((n_lru, 2, D_RNN), 0.02),
        'lru_w_x': nrm((n_lru, 2, LRU_BLOCKS, LRU_BLOCK_W, LRU_BLOCK_W), LRU_BLOCK_W ** -0.5),
        'lru_b_x': nrm((n_lru, 2, D_RNN), 0.02),
        'lru_lambda': jnp.log(p_lam) - jnp.log1p(-p_lam),
        'lru_w_out': nrm((n_lru, D_RNN, d), D_RNN ** -0.5),
        'pool_w': nrm((n_pool, len(POOL_WINDOWS), POOL_GROUP_W, POOL_GROUP_W), POOL_GROUP_W ** -0.5),
        'pool_b': nrm((n_pool, len(POOL_WINDOWS), POOL_GROUP_W), 0.02),
        'pool_scale': 1.0 + nrm((n_pool, d), 0.1),
        'rwkv_mix': jax.random.uniform(next(keys), (n_rwkv, N_TOKEN_MIX, d), jnp.float32),
        'rwkv_w_r': nrm((n_rwkv, d, d), d ** -0.5),
        'rwkv_w_k': nrm((n_rwkv, d, d), d ** -0.5),
        'rwkv_w_v': nrm((n_rwkv, d, d), d ** -0.5),
        'rwkv_w_o': nrm((n_rwkv, d, d), d ** -0.5),
        'rwkv_dec_w0': nrm((n_rwkv, 2, d), 0.5),
        'rwkv_dec_w1': nrm((n_rwkv, 2, d, DECAY_LORA), d ** -0.5),
        'rwkv_dec_w2': nrm((n_rwkv, 2, DECAY_LORA, d), 0.1 * DECAY_LORA ** -0.5),
        'rwkv_a_w0': nrm((n_rwkv, 2, d), 0.1),
        'rwkv_a_w1': nrm((n_rwkv, 2, d, ICLR_LORA), d ** -0.5),
        'rwkv_a_w2': nrm((n_rwkv, 2, ICLR_LORA, d), 0.5 * ICLR_LORA ** -0.5),
        'rwkv_g_w1': nrm((n_rwkv, d, GATE_LORA), d ** -0.5),
        'rwkv_g_w2': nrm((n_rwkv, GATE_LORA, d), GATE_LORA ** -0.5),
        'rwkv_k_k': 0.85 + nrm((n_rwkv, d), 0.02),
        'rwkv_k_a': 1.0 + nrm((n_rwkv, d), 0.02),
        'rwkv_r_k': nrm((n_rwkv, RWKV_HEADS, RWKV_HEAD), 0.1),
        'rwkv_ln_w': gain((n_rwkv, d)),
        'rwkv_ln_b': nrm((n_rwkv, d), 0.02),
        'attn_w_qkv': nrm((n_attn, d, qkv_w), d ** -0.5),
        'attn_b_qkv': nrm((n_attn, qkv_w), 0.02),
        'attn_w_o': nrm((n_attn, N_HEADS * HEAD_DIM, d), (N_HEADS * HEAD_DIM) ** -0.5),
        'attn_b_o': nrm((n_attn, d), 0.02),
        'attn_sinks': nrm((n_attn, N_HEADS), 0.5),
        'moe_w_router': nrm((DEPTH, d, N_EXPERTS), d ** -0.5),
        'moe_b_router': nrm((DEPTH, N_EXPERTS), 0.01),
        'moe_w_gu': nrm((DEPTH, N_EXPERTS, d, 2 * D_EXPERT), d ** -0.5),
        'moe_b_gu': nrm((DEPTH, N_EXPERTS, 2 * D_EXPERT), 0.01),
        'moe_w_down': nrm((DEPTH, N_EXPERTS, D_EXPERT, d), D_EXPERT ** -0.5),
        'moe_b_down': nrm((DEPTH, N_EXPERTS, d), 0.01),
        'final_norm': gain((d,)),
    }


def reference(x_prompt, x_sample, c, c_ctx, state_lru_fwd, state_lru_bwd, state_rwkv_fwd,
              state_rwkv_bwd, cache_attn_k, cache_attn_v, norm_mix, norm_ffn, w_mod, b_mod,
              lru_w_in, lru_conv_w, lru_conv_b, lru_w_a, lru_b_a, lru_w_x, lru_b_x, lru_lambda,
              lru_w_out, pool_w, pool_b, pool_scale, rwkv_mix, rwkv_w_r, rwkv_w_k, rwkv_w_v,
              rwkv_w_o, rwkv_dec_w0, rwkv_dec_w1, rwkv_dec_w2, rwkv_a_w0, rwkv_a_w1, rwkv_a_w2,
              rwkv_g_w1, rwkv_g_w2, rwkv_k_k, rwkv_k_a, rwkv_r_k, rwkv_ln_w, rwkv_ln_b,
              attn_w_qkv, attn_b_qkv, attn_w_o, attn_b_o, attn_sinks, moe_w_router,
              moe_b_router, moe_w_gu, moe_b_gu, moe_w_down, moe_b_down, final_norm):
    f32 = jnp.float32
    bp, sp, _ = x_prompt.shape
    bs, ss, _ = x_sample.shape
    cond_ctx = jax.nn.silu(c_ctx)
    cond_lat = jax.nn.silu(c)
    xp, xs = x_prompt, x_sample
    new_lru_f, new_lru_b, new_rwkv_f, new_rwkv_b, new_k, new_v = [], [], [], [], [], []
    for layer in range(DEPTH):
        kind, j = layer % N_MIXERS, layer // N_MIXERS
        mp = (cond_ctx @ w_mod[layer] + b_mod[layer]).reshape(N_MOD, D_MODEL)
        ms = (cond_lat @ w_mod[layer] + b_mod[layer]).reshape(bs, 1, N_MOD, D_MODEL)
        hp = _modulate(_rmsnorm(xp, norm_mix[layer]), mp[0], mp[1])
        hs = _modulate(_rmsnorm(xs, norm_mix[layer]), ms[:, :, 0], ms[:, :, 1])
        if kind == 0:
            lru = (lru_w_in[j], lru_conv_w[j], lru_conv_b[j], lru_w_a[j], lru_b_a[j],
                   lru_w_x[j], lru_b_x[j], lru_lambda[j], lru_w_out[j])
            zero = jnp.zeros((bp, D_RNN), f32)
            o_p, hf, hb = _rglru_mixer(hp, zero, zero, *lru)
            o_s, _, _ = _rglru_mixer(hs, state_lru_fwd[:, j], state_lru_bwd[:, j], *lru)
            new_lru_f.append(hf.astype(x_prompt.dtype))
            new_lru_b.append(hb.astype(x_prompt.dtype))
        elif kind == 1:
            o_p = _pool_mixer(hp, pool_w[j], pool_b[j], pool_scale[j])
            o_s = _pool_mixer(hs, pool_w[j], pool_b[j], pool_scale[j])
        elif kind == 2:
            rw = (rwkv_mix[j], rwkv_w_r[j], rwkv_w_k[j], rwkv_w_v[j], rwkv_w_o[j],
                  rwkv_dec_w0[j], rwkv_dec_w1[j], rwkv_dec_w2[j], rwkv_a_w0[j], rwkv_a_w1[j],
                  rwkv_a_w2[j], rwkv_g_w1[j], rwkv_g_w2[j], rwkv_k_k[j], rwkv_k_a[j],
                  rwkv_r_k[j], rwkv_ln_w[j], rwkv_ln_b[j])
            zero = jnp.zeros((bp, RWKV_HEADS, RWKV_HEAD, RWKV_HEAD), f32)
            o_p, sf, sb = _rwkv_mixer(hp, zero, zero, *rw)
            o_s, _, _ = _rwkv_mixer(hs, state_rwkv_fwd[:, j], state_rwkv_bwd[:, j], *rw)
            new_rwkv_f.append(sf.astype(x_prompt.dtype))
            new_rwkv_b.append(sb.astype(x_prompt.dtype))
        else:
            at = (attn_w_qkv[j], attn_b_qkv[j], attn_w_o[j], attn_b_o[j], attn_sinks[j])
            o_p, k_ctx, v_ctx = _attn_context(hp, *at)
            o_s = _attn_latent(hs, cache_attn_k[:, j], cache_attn_v[:, j], *at)
            new_k.append(k_ctx)
            new_v.append(v_ctx)
        xp = xp + mp[2] * o_p
        xs = xs + ms[:, :, 2] * o_s
        hp = _modulate(_rmsnorm(xp, norm_ffn[layer]), mp[3], mp[4])
        hs = _modulate(_rmsnorm(xs, norm_ffn[layer]), ms[:, :, 3], ms[:, :, 4])
        tokens = jnp.concatenate([hp.reshape(bp * sp, D_MODEL), hs.reshape(bs * ss, D_MODEL)], axis=0)
        ffn = _moe(tokens, moe_w_router[layer], moe_b_router[layer], moe_w_gu[layer],
                   moe_b_gu[layer], moe_w_down[layer], moe_b_down[layer])
        xp = xp + mp[5] * ffn[:bp * sp].reshape(bp, sp, D_MODEL)
        xs = xs + ms[:, :, 5] * ffn[bp * sp:].reshape(bs, ss, D_MODEL)
    y_prompt = _rmsnorm(xp, final_norm)
    y_sample = _rmsnorm(xs, final_norm)
    new_state_lru_fwd = jnp.stack(new_lru_f, axis=1)
    new_state_lru_bwd = jnp.stack(new_lru_b, axis=1)
    new_state_rwkv_fwd = jnp.stack(new_rwkv_f, axis=1)
    new_state_rwkv_bwd = jnp.stack(new_rwkv_b, axis=1)
    new_cache_attn_k = jnp.stack(new_k, axis=1)
    new_cache_attn_v = jnp.stack(new_v, axis=1)
    return (y_prompt, y_sample, new_state_lru_fwd, new_state_lru_bwd, new_state_rwkv_fwd,
            new_state_rwkv_bwd, new_cache_attn_k, new_cache_attn_v)
```

```python
import functools

import jax
import jax.numpy as jnp
from jax import lax
from jax.experimental import pallas as pl
from jax.experimental.pallas import tpu as pltpu

F32 = jnp.float32
BF16 = jnp.bfloat16
I32 = jnp.int32

RMS_EPS = 1e-6
N_MOD = 6
MOD_ROWS = 8
MOD_GROUPS = 8
LRU_BLOCKS = 8
CONV_W = 4
LRU_C = 8.0
POOL_WINDOWS = (2, 4, 8, 16)
RWKV_HEAD = 64
RWKV_GN_EPS = 64e-5
N_TOKEN_MIX = 6
N_HEADS = 32
N_KV_HEADS = 4
KV_GROUP = N_HEADS // N_KV_HEADS
HEAD_DIM = 64
WINDOW = 128
GRID_W = 64
ROPE_BASE = 10000.0
ATTN_SCALE = HEAD_DIM ** -0.5
NEG_INF = -1e30
N_EXPERTS = 32
TOP_K = 4
SWIGLU_LIMIT = 7.0
SWIGLU_ALPHA = 1.702

LANES = 128
TOKEN_TILE = 256
MOE_TILE = 256
COMBINE_TILE = 64
SCAN_CHUNK = 32
VMEM_LIMIT = 56 * 1024 * 1024

HIGHEST = lax.Precision.HIGHEST


def _params(*sem):
    return pltpu.CompilerParams(dimension_semantics=sem, vmem_limit_bytes=VMEM_LIMIT)


def _group_of(tile, tile_rows, n_ctx_rows, lat_seq):
    row = tile * tile_rows
    return jnp.where(row < n_ctx_rows, 0, 1 + (row - n_ctx_rows) // lat_seq)


def _mod_kernel(cond_ref, w_ref, b_ref, o_ref):
    c = cond_ref[...]
    h = c * jax.nn.sigmoid(c)
    o_ref[...] = jnp.dot(h, w_ref[...], preferred_element_type=F32, precision=HIGHEST) + b_ref[...]


def _modulation(cond, w_mod, b_mod):
    depth, d, n = w_mod.shape
    tn = 1024
    return pl.pallas_call(
        _mod_kernel,
        grid=(depth, n // tn),
        in_specs=[
            pl.BlockSpec((MOD_GROUPS, d), lambda l, j: (0, 0)),
            pl.BlockSpec((None, d, tn), lambda l, j: (l, 0, j)),
            pl.BlockSpec((None, 1, tn), lambda l, j: (l, 0, j)),
        ],
        out_specs=pl.BlockSpec((None, MOD_GROUPS, tn), lambda l, j: (l, 0, j)),
        out_shape=jax.ShapeDtypeStruct((depth, MOD_GROUPS, n), F32),
        compiler_params=_params("parallel", "parallel"),
        name="modulation",
    )(cond, w_mod, b_mod.reshape(depth, 1, n))


def _all_mods(c, c_ctx, w_mod, b_mod):
    depth, d, _ = w_mod.shape
    bs = c.shape[0]
    cond = jnp.concatenate([c_ctx[None], c, jnp.zeros((MOD_GROUPS - 1 - bs, d), F32)], axis=0)
    mods = _modulation(cond, w_mod, b_mod).reshape(depth, MOD_GROUPS, N_MOD, d)
    return jnp.pad(mods, ((0, 0), (0, 0), (0, MOD_ROWS - N_MOD), (0, 0)))


def _normmod_kernel(*refs, shift_row, scale_row, modulate, router):
    x_ref, g_ref = refs[0], refs[1]
    pos = 2
    if modulate:
        mod_ref = refs[pos]
        pos += 1
    if router:
        wr_ref, br_ref = refs[pos], refs[pos + 1]
        pos += 2
    o_ref = refs[pos]
    x = x_ref[...]
    ms = jnp.mean(x * x, axis=-1, keepdims=True)
    y = x * lax.rsqrt(ms + RMS_EPS) * g_ref[...]
    if modulate:
        y = y * (1.0 + mod_ref[scale_row:scale_row + 1, :]) + mod_ref[shift_row:shift_row + 1, :]
    o_ref[...] = y.astype(o_ref.dtype)
    if router:
        l_ref = refs[pos + 1]
        l_ref[...] = jnp.dot(y, wr_ref[...], preferred_element_type=F32, precision=HIGHEST) + br_ref[...]


def _normmod(x, g, mods, *, shift_row=0, scale_row=1, out_dtype=F32, router=None, n_ctx_rows=0, lat_seq=1):
    t, d = x.shape
    tm = TOKEN_TILE
    modulate = mods is not None
    group = functools.partial(_group_of, tile_rows=tm, n_ctx_rows=n_ctx_rows, lat_seq=lat_seq)
    in_specs = [pl.BlockSpec((tm, d), lambda i: (i, 0)), pl.BlockSpec((1, d), lambda i: (0, 0))]
    args = [x, g.reshape(1, d)]
    if modulate:
        in_specs.append(pl.BlockSpec((None, MOD_ROWS, d), lambda i: (group(i), 0, 0)))
        args.append(mods)
    out_specs = pl.BlockSpec((tm, d), lambda i: (i, 0))
    out_shape = jax.ShapeDtypeStruct((t, d), out_dtype)
    if router is not None:
        wr, br = router
        in_specs += [pl.BlockSpec((d, LANES), lambda i: (0, 0)), pl.BlockSpec((1, LANES), lambda i: (0, 0))]
        args += [wr, br]
        out_specs = (out_specs, pl.BlockSpec((tm, LANES), lambda i: (i, 0)))
        out_shape = (out_shape, jax.ShapeDtypeStruct((t, LANES), F32))
    return pl.pallas_call(
        functools.partial(_normmod_kernel, shift_row=shift_row, scale_row=scale_row,
                          modulate=modulate, router=router is not None),
        grid=(t // tm,),
        in_specs=in_specs,
        out_specs=out_specs,
        out_shape=out_shape,
        compiler_params=_params("parallel"),
        name="normmod",
    )(*args)


def _mm_kernel(*refs, out_act, has_bias, has_res, gate_row):
    a_ref, w_ref = refs[0], refs[1]
    pos = 2
    if has_bias:
        b_ref = refs[pos]
        pos += 1
    if has_res:
        x_ref, mod_ref = refs[pos], refs[pos + 1]
        pos += 2
    o_ref, wbf_ref = refs[pos], refs[pos + 1]

    @pl.when(pl.program_id(1) == 0)
    def _():
        wbf_ref[...] = w_ref[...].astype(BF16)

    acc = jnp.dot(a_ref[...].astype(BF16), wbf_ref[...], preferred_element_type=F32)
    if has_bias:
        acc = acc + b_ref[...]
    if out_act == "tanh":
        acc = jnp.tanh(acc)
    elif out_act == "sigmoid":
        acc = jax.nn.sigmoid(acc)
    if has_res:
        acc = x_ref[...] + mod_ref[gate_row:gate_row + 1, :] * acc
    o_ref[...] = acc.astype(o_ref.dtype)


def _matmul(a, w, bias=None, *, out_act=None, out_dtype=F32, res=None, tm=512, tn=512):
    m, k = a.shape
    n = w.shape[1]
    tn = min(tn, n)
    tm = min(tm, m)
    in_specs = [pl.BlockSpec((tm, k), lambda j, i: (i, 0)), pl.BlockSpec((k, tn), lambda j, i: (0, j))]
    args = [a, w]
    if bias is not None:
        in_specs.append(pl.BlockSpec((1, tn), lambda j, i: (0, j)))
        args.append(bias.reshape(1, n))
    gate_row = 0
    if res is not None:
        x, mods, gate_row, n_ctx_rows, lat_seq = res
        group = functools.partial(_group_of, tile_rows=tm, n_ctx_rows=n_ctx_rows, lat_seq=lat_seq)
        in_specs += [pl.BlockSpec((tm, tn), lambda j, i: (i, j)),
                     pl.BlockSpec((None, MOD_ROWS, tn), lambda j, i: (group(i), 0, j))]
        args += [x, mods]
    return pl.pallas_call(
        functools.partial(_mm_kernel, out_act=out_act, has_bias=bias is not None,
                          has_res=res is not None, gate_row=gate_row),
        grid=(n // tn, m // tm),
        in_specs=in_specs,
        out_specs=pl.BlockSpec((tm, tn), lambda j, i: (i, j)),
        out_shape=jax.ShapeDtypeStruct((m, n), out_dtype),
        scratch_shapes=[pltpu.VMEM((k, tn), BF16)],
        compiler_params=_params("parallel", "arbitrary"),
        name="matmul",
    )(*args)


def _softplus(z):
    return jnp.maximum(z, 0.0) + jnp.log1p(jnp.exp(-jnp.abs(z)))


def _neg_expm1(z):
    t = jnp.tanh(0.5 * z)
    return -2.0 * t / (1.0 - t)


def _lru_kernel(gate_ref, xin_ref, cw_ref, cb_ref, wa_ref, ba_ref, wx_ref, bx_ref, lam_ref,
                h0f_ref, h0b_ref, y_ref, hf_ref, hb_ref, a_sc, u_sc, hs_sc):
    seq = xin_ref.shape[0]
    x = xin_ref[...]
    t_idx = lax.broadcasted_iota(I32, x.shape, 0)

    def shifted(off):
        if off == 0:
            return x
        rolled = pltpu.roll(x, (-off) % seq, 0)
        valid = (t_idx + off >= 0) & (t_idx + off < seq)
        return jnp.where(valid, rolled, 0.0)

    xc = cb_ref[...]
    for j in range(CONV_W):
        xc = xc + cw_ref[j:j + 1, :] * shifted(j - CONV_W // 2)
    xcb = xc.astype(BF16)
    for d in range(2):
        r = jax.nn.sigmoid(jnp.dot(xcb, wa_ref[d].astype(BF16), preferred_element_type=F32)
                           + ba_ref[d:d + 1, :])
        i = jax.nn.sigmoid(jnp.dot(xcb, wx_ref[d].astype(BF16), preferred_element_type=F32)
                           + bx_ref[d:d + 1, :])
        log_a = -LRU_C * r * _softplus(-lam_ref[d:d + 1, :])
        a_sc[d] = jnp.exp(log_a)
        u_sc[d] = jnp.sqrt(_neg_expm1(2.0 * log_a)) * (i * xc)

    def step(t, carry):
        hf, hb = carry
        tb = seq - 1 - t
        hf = a_sc[0, pl.ds(t, 1), :] * hf + u_sc[0, pl.ds(t, 1), :]
        hb = a_sc[1, pl.ds(tb, 1), :] * hb + u_sc[1, pl.ds(tb, 1), :]
        hs_sc[0, pl.ds(t, 1), :] = hf
        hs_sc[1, pl.ds(tb, 1), :] = hb
        return hf, hb

    hf, hb = lax.fori_loop(0, seq, step, (h0f_ref[...], h0b_ref[...]))
    hf_ref[...] = hf
    hb_ref[...] = hb
    y_ref[...] = ((hs_sc[0] + hs_sc[1]) * jax.nn.gelu(gate_ref[...])).astype(y_ref.dtype)


def _lru_scan(gx, n_seq, seq, row_off, conv_w, conv_b, w_a, b_a, w_x, b_x, lam, h0f, h0b):
    d = gx.shape[1] // 2
    c = d // LRU_BLOCKS
    seq_spec = lambda col_off: pl.BlockSpec((seq, c), lambda b, n: (b + row_off, n + col_off))
    vec2 = pl.BlockSpec((2, c), lambda b, n: (0, n))
    state = pl.BlockSpec((None, 1, c), lambda b, n: (b, 0, n))
    gate_w = pl.BlockSpec((2, None, c, c), lambda b, n: (0, n, 0, 0))
    return pl.pallas_call(
        _lru_kernel,
        grid=(n_seq, LRU_BLOCKS),
        in_specs=[seq_spec(0), seq_spec(LRU_BLOCKS),
                  pl.BlockSpec((CONV_W, c), lambda b, n: (0, n)),
                  pl.BlockSpec((1, c), lambda b, n: (0, n)),
                  gate_w, vec2, gate_w, vec2, vec2, state, state],
        out_specs=(pl.BlockSpec((seq, c), lambda b, n: (b, n)), state, state),
        out_shape=(jax.ShapeDtypeStruct((n_seq * seq, d), BF16),
                   jax.ShapeDtypeStruct((n_seq, 1, d), F32),
                   jax.ShapeDtypeStruct((n_seq, 1, d), F32)),
        scratch_shapes=[pltpu.VMEM((2, seq, c), F32)] * 3,
        compiler_params=_params("parallel", "parallel"),
        name="rglru",
    )(gx, gx, conv_w, conv_b.reshape(1, d), w_a, b_a, w_x, b_x, lam, h0f, h0b)


POOL_PAD = max(POOL_WINDOWS) // 2


def _pool_kernel(h_ref, w_ref, b_ref, sc_ref, x_ref, mod_ref, o_ref, xe_ref, *, gate_row):
    seq, c = h_ref.shape
    h = h_ref[...]
    zeros = jnp.zeros((POOL_PAD, c), F32)
    xe_ref[0:POOL_PAD, :] = zeros
    xe_ref[POOL_PAD + seq:POOL_PAD + seq + POOL_PAD, :] = zeros
    xe_ref[POOL_PAD:POOL_PAD + seq, :] = h
    t = lax.broadcasted_iota(I32, (seq, c), 0)
    grp = pl.program_id(1)
    for k, win in enumerate(POOL_WINDOWS):
        @pl.when(grp == k)
        def _(win=win):
            left = win // 2
            right = win - 1 - left
            total = xe_ref[POOL_PAD - left:POOL_PAD - left + seq, :]
            for off in range(-left + 1, right + 1):
                total = total + xe_ref[POOL_PAD + off:POOL_PAD + off + seq, :]
            count = (jnp.minimum(t + right, seq - 1) - jnp.maximum(t - left, 0) + 1).astype(F32)
            dpool = (total / count - h).astype(BF16)
            y = jnp.dot(dpool, w_ref[...].astype(BF16), preferred_element_type=F32) + b_ref[...]
            o_ref[...] = x_ref[...] + mod_ref[gate_row:gate_row + 1, :] * (y * sc_ref[...])


def _pool_mixer(h, x, mods, gate_row, w, b, scale, n_seq, seq, row_off):
    d = h.shape[1]
    ng = len(POOL_WINDOWS)
    c = d // ng
    seq_spec = pl.BlockSpec((seq, c), lambda s, g: (s + row_off, g))
    return pl.pallas_call(
        functools.partial(_pool_kernel, gate_row=gate_row),
        grid=(n_seq, ng),
        in_specs=[seq_spec,
                  pl.BlockSpec((None, c, c), lambda s, g: (g, 0, 0)),
                  pl.BlockSpec((None, 1, c), lambda s, g: (g, 0, 0)),
                  pl.BlockSpec((1, c), lambda s, g: (0, g)),
                  seq_spec,
                  pl.BlockSpec((None, MOD_ROWS, c), lambda s, g: (0 if row_off == 0 else 1 + s, 0, g))],
        out_specs=pl.BlockSpec((seq, c), lambda s, g: (s, g)),
        out_shape=jax.ShapeDtypeStruct((n_seq * seq, d), F32),
        scratch_shapes=[pltpu.VMEM((seq + 2 * POOL_PAD, c), F32)],
        compiler_params=_params("parallel", "parallel"),
        name="pool_mixer",
    )(h, w, b.reshape(ng, 1, c), scale.reshape(1, d), x, mods)


def _shiftmix_kernel(h_ref, mix_ref, o_ref):
    seq = h_ref.shape[0]
    h = h_ref[...]
    t = lax.broadcasted_iota(I32, h.shape, 0)
    prev = jnp.where(t >= 1, pltpu.roll(h, 1, 0), 0.0)
    nxt = jnp.where(t < seq - 1, pltpu.roll(h, seq - 1, 0), 0.0)
    xx = 0.5 * (prev + nxt) - h
    for n in range(N_TOKEN_MIX):
        o_ref[n] = (h + xx * mix_ref[n:n + 1, :]).astype(o_ref.dtype)


def _shiftmix(h, mix, n_seq, seq, row_off, c=512):
    t, d = h.shape
    return pl.pallas_call(
        _shiftmix_kernel,
        grid=(n_seq, d // c),
        in_specs=[pl.BlockSpec((seq, c), lambda s, j: (s + row_off, j)),
                  pl.BlockSpec((N_TOKEN_MIX, c), lambda s, j: (0, j))],
        out_specs=pl.BlockSpec((N_TOKEN_MIX, seq, c), lambda s, j: (0, s, j)),
        out_shape=jax.ShapeDtypeStruct((N_TOKEN_MIX, n_seq * seq, d), BF16),
        compiler_params=_params("parallel", "parallel"),
        name="rwkv_shiftmix",
    )(h, mix)


def _head_sums(x):
    r = lax.broadcasted_iota(I32, (LANES, LANES), 0) // RWKV_HEAD
    c = lax.broadcasted_iota(I32, (LANES, LANES), 1) // RWKV_HEAD
    ones = jnp.where(r == c, 1.0, 0.0).astype(BF16)
    outs = []
    for j in range(x.shape[1] // LANES):
        xs = x[:, j * LANES:(j + 1) * LANES]
        hi = xs.astype(BF16)
        rem = xs - hi.astype(F32)
        mid = rem.astype(BF16)
        lo = (rem - mid.astype(F32)).astype(BF16)
        outs.append(jnp.dot(hi, ones, preferred_element_type=F32)
                    + jnp.dot(mid, ones, preferred_element_type=F32)
                    + jnp.dot(lo, ones, preferred_element_type=F32))
    return jnp.concatenate(outs, axis=-1)


def _rwkv_prep_kernel(r_ref, k_ref, v_ref, dec_ref, aa_ref, kk_ref, ka_ref, rk_ref,
                      a_ref, w_ref, kd_ref, b_ref, bonus_ref):
    d = r_ref.shape[1]
    r = r_ref[...]
    k = k_ref[...]
    kk = k * kk_ref[...]
    kk = kk / jnp.maximum(jnp.sqrt(_head_sums(kk * kk)), 1e-12)
    a_ref[...] = -kk
    for dr in range(2):
        w_log = -_softplus(-dec_ref[:, dr * d:(dr + 1) * d]) - 0.5
        w_ref[dr] = jnp.exp(-jnp.exp(w_log))
        iclr = jax.nn.sigmoid(aa_ref[:, dr * d:(dr + 1) * d])
        kd_ref[dr] = k * (1.0 + (iclr - 1.0) * ka_ref[...])
        b_ref[dr] = kk * iclr
    bonus_ref[...] = _head_sums(r * k * rk_ref[...]) * v_ref[...]


def _rwkv_prep(r, k, v, dec, aa, k_k, k_a, r_k):
    t, d = r.shape
    tm = TOKEN_TILE // 2
    row = pl.BlockSpec((tm, d), lambda i: (i, 0))
    row2 = pl.BlockSpec((tm, 2 * d), lambda i: (i, 0))
    vec = pl.BlockSpec((1, d), lambda i: (0, 0))
    both = pl.BlockSpec((2, tm, d), lambda i: (0, i, 0))
    one = jax.ShapeDtypeStruct((t, d), F32)
    two = jax.ShapeDtypeStruct((2, t, d), F32)
    return pl.pallas_call(
        _rwkv_prep_kernel,
        grid=(t // tm,),
        in_specs=[row, row, row, row2, row2, vec, vec, vec],
        out_specs=(row, both, both, both, row),
        out_shape=(one, two, two, two, one),
        compiler_params=_params("parallel"),
        name="rwkv_prep",
    )(r, k, v, dec, aa, k_k.reshape(1, d), k_a.reshape(1, d), r_k.reshape(1, d))


def _rwkv_scan_kernel(r_ref, w_ref, k_ref, a_ref, b_ref, v_ref, s0_ref, y_ref, st_ref, state):
    chunk = pl.program_id(1)
    n_t = r_ref.shape[0]
    n_v = v_ref.shape[1]

    @pl.when(chunk == 0)
    def _():
        state[...] = s0_ref[...]

    def t_step(t, _):
        w_t = w_ref[t]
        a_t = a_ref[t]
        b_t = b_ref[t]
        k_t = k_ref[t]
        r_t = r_ref[t]

        def v_step(v, _):
            s = state[v]
            sa = jnp.sum(s * a_t, axis=0, keepdims=True)
            s = s * w_t + sa * b_t + v_ref[t, pl.ds(v, 1), :] * k_t
            state[v] = s
            y_ref[t, pl.ds(v, 1), :] = jnp.sum(s * r_t, axis=0, keepdims=True)
            return 0

        lax.fori_loop(0, n_v, v_step, 0, unroll=4)
        return 0

    lax.fori_loop(0, n_t, t_step, 0)

    @pl.when(chunk == pl.num_programs(1) - 1)
    def _():
        st_ref[...] = state[...]


def _rwkv_scan(r, w, k, a, b, v, s0):
    seq, n, chains = r.shape
    tc = min(SCAN_CHUNK, seq)
    seq_spec = pl.BlockSpec((tc, n, LANES), lambda g, c: (c, 0, g))
    st_spec = pl.BlockSpec((n, n, LANES), lambda g, c: (0, 0, g))
    return pl.pallas_call(
        _rwkv_scan_kernel,
        grid=(chains // LANES, seq // tc),
        in_specs=[seq_spec] * 6 + [st_spec],
        out_specs=(seq_spec, st_spec),
        out_shape=(jax.ShapeDtypeStruct((seq, n, chains), F32),
                   jax.ShapeDtypeStruct((n, n, chains), F32)),
        scratch_shapes=[pltpu.VMEM((n, n, LANES), F32)],
        compiler_params=_params("parallel", "arbitrary"),
        name="rwkv_scan",
    )(r, w, k, a, b, v, s0)


def _rwkv_post_kernel(yf_ref, yb_ref, bonus_ref, g_ref, lnw_ref, lnb_ref, o_ref):
    y = yf_ref[...] + yb_ref[...]
    mu = _head_sums(y) * (1.0 / RWKV_HEAD)
    yc = y - mu
    var = _head_sums(yc * yc) * (1.0 / RWKV_HEAD)
    y = yc * lax.rsqrt(var + RWKV_GN_EPS) * lnw_ref[...] + lnb_ref[...]
    o_ref[...] = ((y + bonus_ref[...]) * g_ref[...]).astype(o_ref.dtype)


def _rwkv_post(yf, yb, bonus, g, ln_w, ln_b):
    t, d = yf.shape
    tm = TOKEN_TILE
    row = pl.BlockSpec((tm, d), lambda i: (i, 0))
    vec = pl.BlockSpec((1, d), lambda i: (0, 0))
    return pl.pallas_call(
        _rwkv_post_kernel,
        grid=(t // tm,),
        in_specs=[row, row, row, row, vec, vec],
        out_specs=row,
        out_shape=jax.ShapeDtypeStruct((t, d), BF16),
        compiler_params=_params("parallel"),
        name="rwkv_post",
    )(yf, yb, bonus, g, ln_w.reshape(1, d), ln_b.reshape(1, d))


def _rope_kernel(x_ref, cos_ref, sin_ref, o_ref):
    x = x_ref[...]
    hd = x.shape[1]
    quarter = hd // 4
    src = lax.broadcasted_iota(I32, (hd, hd), 0)
    dst = lax.broadcasted_iota(I32, (hd, hd), 1)
    partner = jnp.where((dst // quarter) % 2 == 0, dst + quarter, dst - quarter)
    perm = jnp.where(src == partner, 1.0, 0.0).astype(BF16)
    hi = x.astype(BF16)
    rem = x - hi.astype(F32)
    mid = rem.astype(BF16)
    lo = (rem - mid.astype(F32)).astype(BF16)
    swapped = (jnp.dot(hi, perm, preferred_element_type=F32)
               + jnp.dot(mid, perm, preferred_element_type=F32)
               + jnp.dot(lo, perm, preferred_element_type=F32))
    o_ref[...] = (x * cos_ref[...] + swapped * sin_ref[...]).astype(o_ref.dtype)


def _rope(x, cos, sin):
    n, seq, hd = x.shape
    tab = pl.BlockSpec((seq, hd), lambda i: (0, 0))
    return pl.pallas_call(
        _rope_kernel,
        grid=(n,),
        in_specs=[pl.BlockSpec((None, seq, hd), lambda i: (i, 0, 0)), tab, tab],
        out_specs=pl.BlockSpec((None, seq, hd), lambda i: (i, 0, 0)),
        out_shape=jax.ShapeDtypeStruct((n, seq, hd), BF16),
        compiler_params=_params("parallel"),
        name="rope",
    )(x, cos, sin)


def _attn_kernel(sink_ref, q_ref, k_ref, v_ref, o_ref, *, n_cache, window):
    tq = q_ref.shape[0]
    n_k = k_ref.shape[0]
    sink = sink_ref[pl.program_id(1) * KV_GROUP + pl.program_id(2)]
    s = lax.dot_general(q_ref[...].astype(BF16), k_ref[...].astype(BF16),
                        (((1,), (1,)), ((), ())), preferred_element_type=F32) * ATTN_SCALE
    if window is not None:
        q_pos = pl.program_id(3) * tq + lax.broadcasted_iota(I32, (tq, n_k), 0)
        col = lax.broadcasted_iota(I32, (tq, n_k), 1)
        valid = (col < n_cache) | (jnp.abs(q_pos - (col - n_cache)) <= window)
        s = jnp.where(valid, s, NEG_INF)
    m = jnp.maximum(jnp.max(s, axis=-1, keepdims=True), sink)
    p = jnp.exp(s - m)
    denom = jnp.sum(p, axis=-1, keepdims=True) + jnp.exp(sink - m)
    o = jnp.dot(p.astype(BF16), v_ref[...].astype(BF16), preferred_element_type=F32)
    o_ref[...] = (o / denom).astype(o_ref.dtype)


def _attention(q, k, v, sinks, *, n_cache=0, window=None, tq=256):
    b, kv, g, sq, hd = q.shape
    sk = k.shape[2]
    kv_spec = pl.BlockSpec((None, None, sk, hd), lambda bi, ki, gi, qi, s: (bi, ki, 0, 0))
    q_spec = pl.BlockSpec((None, None, None, tq, hd), lambda bi, ki, gi, qi, s: (bi, ki, gi, qi, 0))
    return pl.pallas_call(
        functools.partial(_attn_kernel, n_cache=n_cache, window=window),
        grid_spec=pltpu.PrefetchScalarGridSpec(
            num_scalar_prefetch=1,
            grid=(b, kv, g, sq // tq),
            in_specs=[q_spec, kv_spec, kv_spec],
            out_specs=q_spec),
        out_shape=jax.ShapeDtypeStruct(q.shape, BF16),
        compiler_params=_params("parallel", "parallel", "parallel", "parallel"),
        name="attention",
    )(sinks, q, k, v)


def _rope_tables(seq):
    quarter = HEAD_DIM // 4
    pos = jnp.arange(seq)
    rows = (pos // GRID_W).astype(F32)
    cols = (pos % GRID_W).astype(F32)
    inv_freq = ROPE_BASE ** (-jnp.arange(quarter, dtype=F32) / quarter)
    ang_r = rows[:, None] * inv_freq
    ang_c = cols[:, None] * inv_freq
    cos = jnp.concatenate([jnp.cos(ang_r)] * 2 + [jnp.cos(ang_c)] * 2, axis=1)
    sin = jnp.concatenate([-jnp.sin(ang_r), jnp.sin(ang_r), -jnp.sin(ang_c), jnp.sin(ang_c)], axis=1)
    return cos, sin


def _topk_kernel(l_ref, e_ref, g_ref):
    x = l_ref[...]
    lane = lax.broadcasted_iota(I32, x.shape, 1)
    lane_f = lane.astype(F32)
    x = jnp.where(lane < N_EXPERTS, x, -jnp.inf)
    e_out = jnp.zeros(x.shape, F32)
    tops = []
    for r in range(TOP_K):
        m = jnp.max(x, axis=-1, keepdims=True)
        idx = jnp.min(jnp.where(x == m, lane_f, float(LANES)), axis=-1, keepdims=True)
        e_out = jnp.where(lane == r, idx, e_out)
        x = jnp.where(lane_f == idx, -jnp.inf, x)
        tops.append(m)
    ex = [jnp.exp(m - tops[0]) for m in tops]
    tot = ex[0] + ex[1] + ex[2] + ex[3]
    g_out = jnp.zeros(x.shape, F32)
    for r in range(TOP_K):
        g_out = jnp.where(lane == r, ex[r] / tot, g_out)
    e_ref[...] = e_out.astype(I32)
    g_ref[...] = g_out


def _topk(logits):
    t = logits.shape[0]
    tm = TOKEN_TILE
    spec = pl.BlockSpec((tm, LANES), lambda i: (i, 0))
    return pl.pallas_call(
        _topk_kernel,
        grid=(t // tm,),
        in_specs=[spec],
        out_specs=(spec, spec),
        out_shape=(jax.ShapeDtypeStruct((t, LANES), I32), jax.ShapeDtypeStruct((t, LANES), F32)),
        compiler_params=_params("parallel"),
        name="router_topk",
    )(logits)


def _rank_kernel(e_ref, rank_ref, cnt_ref, carry):
    @pl.when(pl.program_id(0) == 0)
    def _():
        carry[...] = jnp.zeros(carry.shape, F32)

    e = e_ref[...]
    tm = e.shape[0]
    lane = lax.broadcasted_iota(I32, e.shape, 1)
    hits = []
    member = jnp.zeros(e.shape, F32)
    for r in range(TOP_K):
        hit = lane == e[:, r:r + 1]
        hits.append(hit)
        member = member + jnp.where(hit, 1.0, 0.0)
    row = lax.broadcasted_iota(I32, (tm, tm), 0)
    col = lax.broadcasted_iota(I32, (tm, tm), 1)
    tri = jnp.where(row > col, 1.0, 0.0).astype(BF16)
    before = jnp.dot(tri, member.astype(BF16), preferred_element_type=F32) + carry[0:1, :]
    out = jnp.zeros(e.shape, F32)
    for r in range(TOP_K):
        rk = jnp.sum(jnp.where(hits[r], before, 0.0), axis=-1, keepdims=True)
        out = jnp.where(lane == r, rk, out)
    rank_ref[...] = out.astype(I32)
    carry[...] = carry[...] + jnp.sum(member, axis=0, keepdims=True)
    cnt_ref[...] = carry[...]


def _rank(top_e):
    t = top_e.shape[0]
    tm = TOKEN_TILE
    spec = pl.BlockSpec((tm, LANES), lambda i: (i, 0))
    cnt_spec = pl.BlockSpec((8, LANES), lambda i: (0, 0))
    return pl.pallas_call(
        _rank_kernel,
        grid=(t // tm,),
        in_specs=[spec],
        out_specs=(spec, cnt_spec),
        out_shape=(jax.ShapeDtypeStruct((t, LANES), I32), jax.ShapeDtypeStruct((8, LANES), F32)),
        scratch_shapes=[pltpu.VMEM((8, LANES), F32)],
        compiler_params=_params("arbitrary"),
        name="router_rank",
    )(top_e)


def _row_copy(src_hbm, row, buf, slot, sem):
    return pltpu.make_async_copy(src_hbm.at[pl.ds(row, 1), :], buf.at[pl.ds(slot, 1), :], sem)


def _gather_kernel(idx_ref, src_hbm, o_ref, buf, sem):
    n = o_ref.shape[0]
    base = pl.program_id(0) * n

    def issue(r, _):
        _row_copy(src_hbm, idx_ref[base + r], buf, r, sem).start()
        return 0

    lax.fori_loop(0, n, issue, 0)

    def drain(r, _):
        _row_copy(src_hbm, 0, buf, r, sem).wait()
        return 0

    lax.fori_loop(0, n, drain, 0)
    o_ref[...] = buf[...].astype(o_ref.dtype)


def _gather_rows(src, idx, out_dtype):
    n = idx.shape[0]
    d = src.shape[1]
    tm = MOE_TILE
    return pl.pallas_call(
        _gather_kernel,
        grid_spec=pltpu.PrefetchScalarGridSpec(
            num_scalar_prefetch=1,
            grid=(n // tm,),
            in_specs=[pl.BlockSpec(memory_space=pl.ANY)],
            out_specs=pl.BlockSpec((tm, d), lambda i, idx: (i, 0)),
            scratch_shapes=[pltpu.VMEM((tm, d), src.dtype), pltpu.SemaphoreType.DMA(())]),
        out_shape=jax.ShapeDtypeStruct((n, d), out_dtype),
        compiler_params=_params("arbitrary"),
        name="moe_gather",
    )(idx, src)


def _gmm_kernel(blk_e_ref, first_ref, nused_ref, *refs, swiglu):
    if swiglu:
        x_ref, wg_ref, wu_ref, bg_ref, bu_ref, o_ref, wg_sc, wu_sc = refs
    else:
        x_ref, wg_ref, bg_ref, o_ref, wg_sc = refs
    i = pl.program_id(1)

    @pl.when(i < nused_ref[0])
    def _():
        @pl.when(first_ref[i] == 1)
        def _():
            wg_sc[...] = wg_ref[...].astype(BF16)
            if swiglu:
                wu_sc[...] = wu_ref[...].astype(BF16)

        x = x_ref[...]
        g = jnp.dot(x, wg_sc[...], preferred_element_type=F32) + bg_ref[...]
        if swiglu:
            u = jnp.dot(x, wu_sc[...], preferred_element_type=F32) + bu_ref[...]
            g = jnp.minimum(g, SWIGLU_LIMIT)
            u = jnp.clip(u, -SWIGLU_LIMIT, SWIGLU_LIMIT)
            g = (u + 1.0) * (g * jax.nn.sigmoid(SWIGLU_ALPHA * g))
        o_ref[...] = g.astype(o_ref.dtype)

    @pl.when(i >= nused_ref[0])
    def _():
        o_ref[...] = jnp.zeros(o_ref.shape, o_ref.dtype)


def _grouped_matmul(x, w, b, blk_e, first, n_used, *, swiglu, out_dtype, tn=512):
    p, k = x.shape
    n_e = w.shape[0]
    n = w.shape[2] // 2 if swiglu else w.shape[2]
    tm = MOE_TILE
    nj = n // tn
    b3 = b.reshape(n_e, 1, b.shape[1])
    x_spec = pl.BlockSpec((tm, k), lambda j, i, be, fi, nu: (jnp.minimum(i, nu[0] - 1), 0))
    w_spec = lambda off: pl.BlockSpec((None, k, tn), lambda j, i, be, fi, nu: (be[i], 0, j + off))
    b_spec = lambda off: pl.BlockSpec((None, 1, tn), lambda j, i, be, fi, nu: (be[i], 0, j + off))
    if swiglu:
        in_specs = [x_spec, w_spec(0), w_spec(nj), b_spec(0), b_spec(nj)]
        args = (x, w, w, b3, b3)
        scratch = [pltpu.VMEM((k, tn), BF16)] * 2
    else:
        in_specs = [x_spec, w_spec(0), b_spec(0)]
        args = (x, w, b3)
        scratch = [pltpu.VMEM((k, tn), BF16)]
    return pl.pallas_call(
        functools.partial(_gmm_kernel, swiglu=swiglu),
        grid_spec=pltpu.PrefetchScalarGridSpec(
            num_scalar_prefetch=3,
            grid=(nj, p // tm),
            in_specs=in_specs,
            out_specs=pl.BlockSpec((tm, tn), lambda j, i, be, fi, nu: (i, j)),
            scratch_shapes=scratch),
        out_shape=jax.ShapeDtypeStruct((p, n), out_dtype),
        compiler_params=_params("parallel", "arbitrary"),
        name="moe_experts",
    )(blk_e, first, n_used, *args)


def _combine_kernel(dest_ref, ys_hbm, gate_ref, x_ref, mod_ref, o_ref, buf, sem, *, gate_row):
    tm = x_ref.shape[0]
    n = tm * TOP_K
    base = pl.program_id(0) * n

    def issue(r, _):
        _row_copy(ys_hbm, dest_ref[base + r], buf, r, sem).start()
        return 0

    lax.fori_loop(0, n, issue, 0)

    def drain(r, _):
        _row_copy(ys_hbm, 0, buf, r, sem).wait()
        return 0

    lax.fori_loop(0, n, drain, 0)
    acc = gate_ref[:, 0:1] * buf[0:tm, :]
    for k in range(1, TOP_K):
        acc = acc + gate_ref[:, k:k + 1] * buf[k * tm:(k + 1) * tm, :]
    o_ref[...] = x_ref[...] + mod_ref[gate_row:gate_row + 1, :] * acc


def _combine(ys, dest_blocked, gates, x, mods, gate_row, n_ctx_rows, lat_seq):
    t, d = x.shape
    tm = COMBINE_TILE
    group = functools.partial(_group_of, tile_rows=tm, n_ctx_rows=n_ctx_rows, lat_seq=lat_seq)
    return pl.pallas_call(
        functools.partial(_combine_kernel, gate_row=gate_row),
        grid_spec=pltpu.PrefetchScalarGridSpec(
            num_scalar_prefetch=1,
            grid=(t // tm,),
            in_specs=[pl.BlockSpec(memory_space=pl.ANY),
                      pl.BlockSpec((tm, LANES), lambda i, de: (i, 0)),
                      pl.BlockSpec((tm, d), lambda i, de: (i, 0)),
                      pl.BlockSpec((None, MOD_ROWS, d), lambda i, de: (group(i), 0, 0))],
            out_specs=pl.BlockSpec((tm, d), lambda i, de: (i, 0)),
            scratch_shapes=[pltpu.VMEM((tm * TOP_K, d), F32), pltpu.SemaphoreType.DMA(())]),
        out_shape=jax.ShapeDtypeStruct((t, d), F32),
        compiler_params=_params("arbitrary"),
        name="moe_combine",
    )(dest_blocked, ys, gates, x, mods)


def _moe_layer(x, g_norm, mods, w_router, b_router, w_gu, b_gu, w_down, b_down, n_ctx_rows, lat_seq):
    t, d = x.shape
    n_e = w_router.shape[1]
    wr = jnp.pad(w_router, ((0, 0), (0, LANES - n_e)))
    br = jnp.pad(b_router, (0, LANES - n_e)).reshape(1, LANES)
    h, logits = _normmod(x, g_norm, mods, shift_row=3, scale_row=4, router=(wr, br),
                         n_ctx_rows=n_ctx_rows, lat_seq=lat_seq)
    top_e, gates = _topk(logits)
    rank, counts = _rank(top_e)

    counts = counts[0, :n_e].astype(I32)
    blocks_per_e = (counts + MOE_TILE - 1) // MOE_TILE
    blk_end = jnp.cumsum(blocks_per_e)
    seg_start = (blk_end - blocks_per_e) * MOE_TILE
    n_blocks = -(-(t * TOP_K) // MOE_TILE) + n_e
    n_used = blk_end[-1]
    blk_ids = jnp.minimum(jnp.arange(n_blocks, dtype=I32), n_used - 1)
    blk_e = jnp.minimum(jnp.searchsorted(blk_end, blk_ids, side="right"), n_e - 1).astype(I32)
    first = jnp.concatenate([jnp.ones((1,), I32), (blk_e[1:] != blk_e[:-1]).astype(I32)])
    e4 = top_e[:, :TOP_K]
    dest = seg_start[e4] + rank[:, :TOP_K]
    tok = jnp.broadcast_to(jnp.arange(t, dtype=I32)[:, None], (t, TOP_K))
    src_tok = jnp.zeros((n_blocks * MOE_TILE,), I32).at[dest.reshape(-1)].set(tok.reshape(-1))
    dest_blocked = dest.reshape(t // COMBINE_TILE, COMBINE_TILE, TOP_K).transpose(0, 2, 1).reshape(-1)

    xs = _gather_rows(h, src_tok, BF16)
    act = _grouped_matmul(xs, w_gu, b_gu, blk_e, first, n_used.reshape(1), swiglu=True, out_dtype=BF16)
    ys = _grouped_matmul(act, w_down, b_down, blk_e, first, n_used.reshape(1), swiglu=False, out_dtype=F32)
    return _combine(ys, dest_blocked, gates, x, mods, 5, n_ctx_rows, lat_seq)


def _dims(dims):
    bp, sp, bs, ss = dims
    n_ctx = bp * sp
    return n_ctx, n_ctx // ss, dict(n_ctx_rows=n_ctx, lat_seq=ss)


def _lru_layer(x, mods, dims, g_norm, w_in, conv_w, conv_b, w_a, b_a, w_x, b_x, lam, w_out, h0f, h0b):
    bp, sp, bs, ss = dims
    n_ctx, lat_off, grp = _dims(dims)
    d = x.shape[1]
    h = _normmod(x, g_norm, mods, out_dtype=BF16, **grp)
    gx = _matmul(h, w_in)
    lru = (conv_w, conv_b, w_a, b_a, w_x, b_x, lam)
    zero = jnp.zeros((bp, 1, d), F32)
    y_c, hf, hb = _lru_scan(gx, bp, sp, 0, *lru, zero, zero)
    y_l, _, _ = _lru_scan(gx, bs, ss, lat_off, *lru, h0f, h0b)
    y = jnp.concatenate([y_c, y_l], axis=0)
    return _matmul(y, w_out, res=(x, mods, 2, n_ctx, ss)), hf, hb


def _pool_layer(x, mods, dims, g_norm, w, b, scale):
    bp, sp, bs, ss = dims
    n_ctx, lat_off, grp = _dims(dims)
    h = _normmod(x, g_norm, mods, **grp)
    x_c = _pool_mixer(h, x, mods, 2, w, b, scale, bp, sp, 0)
    x_l = _pool_mixer(h, x, mods, 2, w, b, scale, bs, ss, lat_off)
    return jnp.concatenate([x_c, x_l], axis=0)


def _rwkv_layer(x, mods, dims, g_norm, mix, w_r, w_k, w_v, w_o, dec_w0, dec_w1, dec_w2, a_w0, a_w1, a_w2,
                g_w1, g_w2, k_k, k_a, r_k, ln_w, ln_b, s0f, s0b):
    bp, sp, bs, ss = dims
    n_ctx, lat_off, grp = _dims(dims)
    d = x.shape[1]
    heads = d // RWKV_HEAD
    h = _normmod(x, g_norm, mods, **grp)
    x6 = jnp.concatenate([_shiftmix(h, mix, bp, sp, 0), _shiftmix(h, mix, bs, ss, lat_off)], axis=1)
    xr, xw, xk, xv, xa, xg = (x6[n] for n in range(N_TOKEN_MIX))
    r = _matmul(xr, w_r)
    k = _matmul(xk, w_k)
    v = _matmul(xv, w_v)
    pad = 2 * LANES - 2 * dec_w1.shape[-1]

    def lora_in(w1):
        return jnp.pad(jnp.concatenate([w1[0], w1[1]], axis=1), ((0, 0), (0, pad)))

    def lora_out(w2):
        z = jnp.zeros_like(w2[0])
        top = jnp.concatenate([w2[0], z], axis=1)
        bot = jnp.concatenate([z, w2[1]], axis=1)
        return jnp.pad(jnp.concatenate([top, bot], axis=0), ((0, pad), (0, 0)))

    dec_h = _matmul(xw, lora_in(dec_w1), out_act="tanh", out_dtype=BF16)
    dec = _matmul(dec_h, lora_out(dec_w2), dec_w0.reshape(-1))
    aa_h = _matmul(xa, lora_in(a_w1), out_dtype=BF16)
    aa = _matmul(aa_h, lora_out(a_w2), a_w0.reshape(-1))
    g_h = _matmul(xg, g_w1, out_act="sigmoid", out_dtype=BF16)
    g = _matmul(g_h, g_w2)
    a_neg, w_dec, k_d, b_d, bonus = _rwkv_prep(r, k, v, dec, aa, k_k, k_a, r_k.reshape(-1))

    def scan_group(lo, n_seq, seq, s0_fwd, s0_bwd):
        def lay(z, flip):
            z = z[lo:lo + n_seq * seq].reshape(n_seq, seq, heads, RWKV_HEAD)
            z = z.transpose(1, 3, 0, 2).reshape(seq, RWKV_HEAD, n_seq * heads)
            return z[::-1] if flip else z

        both = lambda zf, zb: jnp.concatenate([lay(zf, False), lay(zb, True)], axis=-1)
        st = lambda s: s.transpose(2, 3, 0, 1).reshape(RWKV_HEAD, RWKV_HEAD, n_seq * heads)
        y, s_fin = _rwkv_scan(both(r, r), both(w_dec[0], w_dec[1]), both(k_d[0], k_d[1]),
                              both(a_neg, a_neg), both(b_d[0], b_d[1]), both(v, v),
                              jnp.concatenate([st(s0_fwd), st(s0_bwd)], axis=-1))
        nc = n_seq * heads

        def unlay(z):
            return z.reshape(seq, RWKV_HEAD, n_seq, heads).transpose(2, 0, 3, 1).reshape(n_seq * seq, d)

        unst = lambda s: s.reshape(RWKV_HEAD, RWKV_HEAD, n_seq, heads).transpose(2, 3, 0, 1)
        return unlay(y[:, :, :nc]), unlay(y[::-1, :, nc:]), unst(s_fin[:, :, :nc]), unst(s_fin[:, :, nc:])

    zero = jnp.zeros((bp, heads, RWKV_HEAD, RWKV_HEAD), F32)
    yf_c, yb_c, sf, sb = scan_group(0, bp, sp, zero, zero)
    yf_l, yb_l, _, _ = scan_group(n_ctx, bs, ss, s0f, s0b)
    yf = jnp.concatenate([yf_c, yf_l], axis=0)
    yb = jnp.concatenate([yb_c, yb_l], axis=0)
    yg = _rwkv_post(yf, yb, bonus, g, ln_w, ln_b)
    return _matmul(yg, w_o, res=(x, mods, 2, n_ctx, ss)), sf, sb


def _attn_layer(x, mods, dims, g_norm, w_qkv, b_qkv, w_o, b_o, sinks, cache_k, cache_v):
    bp, sp, bs, ss = dims
    n_ctx, lat_off, grp = _dims(dims)
    h = _normmod(x, g_norm, mods, out_dtype=BF16, **grp)
    qkv = _matmul(h, w_qkv, b_qkv)
    nq = N_HEADS * HEAD_DIM
    nk = N_KV_HEADS * HEAD_DIM

    def split(lo, n_seq, seq):
        z = qkv[lo:lo + n_seq * seq]
        q = z[:, :nq].reshape(n_seq, seq, N_KV_HEADS, KV_GROUP, HEAD_DIM).transpose(0, 2, 3, 1, 4)
        kk = z[:, nq:nq + nk].reshape(n_seq, seq, N_KV_HEADS, HEAD_DIM)
        vv = z[:, nq + nk:].reshape(n_seq, seq, N_KV_HEADS, HEAD_DIM)
        return q, kk, vv

    def merge(o):
        n_seq, _, _, seq, _ = o.shape
        return o.transpose(0, 3, 1, 2, 4).reshape(n_seq * seq, nq)

    q_c, k_c, v_c = split(0, bp, sp)
    o_c = _attention(q_c, k_c.transpose(0, 2, 1, 3), v_c.transpose(0, 2, 1, 3), sinks)
    q_l, k_l, v_l = split(n_ctx, bs, ss)
    cos, sin = _rope_tables(ss)
    q_l = _rope(q_l.reshape(-1, ss, HEAD_DIM), cos, sin).reshape(q_l.shape)
    k_l = _rope(k_l.transpose(0, 2, 1, 3).reshape(-1, ss, HEAD_DIM), cos, sin)
    k_l = k_l.reshape(bs, N_KV_HEADS, ss, HEAD_DIM)
    k_all = jnp.concatenate([cache_k.transpose(0, 2, 1, 3).astype(BF16), k_l], axis=2)
    v_all = jnp.concatenate([cache_v.transpose(0, 2, 1, 3), v_l.transpose(0, 2, 1, 3)], axis=2)
    o_l = _attention(q_l, k_all, v_all, sinks, n_cache=cache_k.shape[1], window=WINDOW)
    o = jnp.concatenate([merge(o_c), merge(o_l)], axis=0)
    return _matmul(o, w_o, b_o, res=(x, mods, 2, n_ctx, ss)), k_c, v_c


def kernel(x_prompt, x_sample, c, c_ctx, state_lru_fwd, state_lru_bwd, state_rwkv_fwd, state_rwkv_bwd, cache_attn_k, cache_attn_v, norm_mix, norm_ffn, w_mod, b_mod, lru_w_in, lru_conv_w, lru_conv_b, lru_w_a, lru_b_a, lru_w_x, lru_b_x, lru_lambda, lru_w_out, pool_w, pool_b, pool_scale, rwkv_mix, rwkv_w_r, rwkv_w_k, rwkv_w_v, rwkv_w_o, rwkv_dec_w0, rwkv_dec_w1, rwkv_dec_w2, rwkv_a_w0, rwkv_a_w1, rwkv_a_w2, rwkv_g_w1, rwkv_g_w2, rwkv_k_k, rwkv_k_a, rwkv_r_k, rwkv_ln_w, rwkv_ln_b, attn_w_qkv, attn_b_qkv, attn_w_o, attn_b_o, attn_sinks, moe_w_router, moe_b_router, moe_w_gu, moe_b_gu, moe_w_down, moe_b_down, final_norm):
    bp, sp, d = x_prompt.shape
    bs, ss, _ = x_sample.shape
    depth = w_mod.shape[0]
    n_ctx = bp * sp
    assert 1 + bs <= MOD_GROUPS and sp % TOKEN_TILE == 0 and ss % 1024 == 0 and n_ctx % ss == 0
    dims = (bp, sp, bs, ss)

    x = jnp.concatenate([x_prompt.reshape(n_ctx, d), x_sample.reshape(bs * ss, d)], axis=0)
    mods_all = _all_mods(c, c_ctx, w_mod, b_mod)

    lru_f, lru_b, rwkv_f, rwkv_b, new_k, new_v = [], [], [], [], [], []
    for layer in range(depth):
        kind, j = layer % 4, layer // 4
        mods = mods_all[layer]
        if kind == 0:
            x, hf, hb = _lru_layer(x, mods, dims, norm_mix[layer], lru_w_in[j], lru_conv_w[j], lru_conv_b[j],
                                   lru_w_a[j], lru_b_a[j], lru_w_x[j], lru_b_x[j], lru_lambda[j], lru_w_out[j],
                                   state_lru_fwd[:, j:j + 1], state_lru_bwd[:, j:j + 1])
            lru_f.append(hf)
            lru_b.append(hb)
        elif kind == 1:
            x = _pool_layer(x, mods, dims, norm_mix[layer], pool_w[j], pool_b[j], pool_scale[j])
        elif kind == 2:
            x, sf, sb = _rwkv_layer(x, mods, dims, norm_mix[layer], rwkv_mix[j], rwkv_w_r[j], rwkv_w_k[j],
                                    rwkv_w_v[j], rwkv_w_o[j], rwkv_dec_w0[j], rwkv_dec_w1[j], rwkv_dec_w2[j],
                                    rwkv_a_w0[j], rwkv_a_w1[j], rwkv_a_w2[j], rwkv_g_w1[j], rwkv_g_w2[j],
                                    rwkv_k_k[j], rwkv_k_a[j], rwkv_r_k[j], rwkv_ln_w[j], rwkv_ln_b[j],
                                    state_rwkv_fwd[:, j], state_rwkv_bwd[:, j])
            rwkv_f.append(sf)
            rwkv_b.append(sb)
        else:
            x, k_c, v_c = _attn_layer(x, mods, dims, norm_mix[layer], attn_w_qkv[j], attn_b_qkv[j], attn_w_o[j],
                                      attn_b_o[j], attn_sinks[j], cache_attn_k[:, j], cache_attn_v[:, j])
            new_k.append(k_c)
            new_v.append(v_c)
        x = _moe_layer(x, norm_ffn[layer], mods, moe_w_router[layer], moe_b_router[layer], moe_w_gu[layer],
                       moe_b_gu[layer], moe_w_down[layer], moe_b_down[layer], n_ctx, ss)

    y = _normmod(x, final_norm, None)
    return (y[:n_ctx].reshape(bp, sp, d), y[n_ctx:].reshape(bs, ss, d),
            jnp.concatenate(lru_f, axis=1), jnp.concatenate(lru_b, axis=1),
            jnp.stack(rwkv_f, axis=1), jnp.stack(rwkv_b, axis=1),
            jnp.stack(new_k, axis=1), jnp.stack(new_v, axis=1))
```

```python
import functools

import jax
import jax.numpy as jnp
from jax import lax
from jax.experimental import pallas as pl
from jax.experimental.pallas import tpu as pltpu

F32 = jnp.float32
BF16 = jnp.bfloat16
I32 = jnp.int32

RMS_EPS = 1e-6
N_MOD = 6
MOD_ROWS = 8
MOD_GROUPS = 8
LRU_BLOCKS = 8
CONV_W = 4
LRU_C = 8.0
POOL_WINDOWS = (2, 4, 8, 16)
RWKV_HEAD = 64
RWKV_GN_EPS = 64e-5
N_TOKEN_MIX = 6
N_HEADS = 32
N_KV_HEADS = 4
KV_GROUP = N_HEADS // N_KV_HEADS
HEAD_DIM = 64
WINDOW = 128
GRID_W = 64
ROPE_BASE = 10000.0
ATTN_SCALE = HEAD_DIM ** -0.5
NEG_INF = -1e30
N_EXPERTS = 32
TOP_K = 4
SWIGLU_LIMIT = 7.0
SWIGLU_ALPHA = 1.702

LANES = 128
TOKEN_TILE = 256
MOE_TILE = 256
COMBINE_TILE = 64
SCAN_CHUNK = 32
VMEM_LIMIT = 56 * 1024 * 1024

HIGHEST = lax.Precision.HIGHEST


def _params(*sem):
    return pltpu.CompilerParams(dimension_semantics=sem, vmem_limit_bytes=VMEM_LIMIT)


def _group_of(tile, tile_rows, n_ctx_rows, lat_seq):
    row = tile * tile_rows
    return jnp.where(row < n_ctx_rows, 0, 1 + (row - n_ctx_rows) // lat_seq)


def _mod_kernel(cond_ref, w_ref, b_ref, o_ref):
    c = cond_ref[...]
    h = c * jax.nn.sigmoid(c)
    o_ref[...] = jnp.dot(h, w_ref[...], preferred_element_type=F32, precision=HIGHEST) + b_ref[...]


def _modulation(cond, w_mod, b_mod):
    depth, d, n = w_mod.shape
    tn = 1024
    return pl.pallas_call(
        _mod_kernel,
        grid=(depth, n // tn),
        in_specs=[
            pl.BlockSpec((MOD_GROUPS, d), lambda l, j: (0, 0)),
            pl.BlockSpec((None, d, tn), lambda l, j: (l, 0, j)),
            pl.BlockSpec((None, 1, tn), lambda l, j: (l, 0, j)),
        ],
        out_specs=pl.BlockSpec((None, MOD_GROUPS, tn), lambda l, j: (l, 0, j)),
        out_shape=jax.ShapeDtypeStruct((depth, MOD_GROUPS, n), F32),
        compiler_params=_params("parallel", "parallel"),
        name="modulation",
    )(cond, w_mod, b_mod.reshape(depth, 1, n))


def _all_mods(c, c_ctx, w_mod, b_mod):
    depth, d, _ = w_mod.shape
    bs = c.shape[0]
    cond = jnp.concatenate([c_ctx[None], c, jnp.zeros((MOD_GROUPS - 1 - bs, d), F32)], axis=0)
    mods = _modulation(cond, w_mod, b_mod).reshape(depth, MOD_GROUPS, N_MOD, d)
    return jnp.pad(mods, ((0, 0), (0, 0), (0, MOD_ROWS - N_MOD), (0, 0)))


def _normmod_kernel(*refs, shift_row, scale_row, modulate, router, slab):
    x_ref, g_ref = refs[0], refs[1]
    pos = 2
    if modulate:
        mod_ref = refs[pos]
        pos += 1
    if router:
        wr_ref, br_ref = refs[pos], refs[pos + 1]
        pos += 2
    o_ref = refs[pos]
    x = x_ref[...]
    ms = jnp.mean(x * x, axis=-1, keepdims=True)
    y = x * lax.rsqrt(ms + RMS_EPS) * g_ref[...]
    if modulate:
        y = y * (1.0 + mod_ref[scale_row:scale_row + 1, :]) + mod_ref[shift_row:shift_row + 1, :]
    if slab:
        n_chunks = y.shape[1] // LANES
        for c in range(n_chunks):
            o_ref[pl.ds(c, y.shape[0], stride=n_chunks), :] = y[:, c * LANES:(c + 1) * LANES]
    else:
        o_ref[...] = y.astype(o_ref.dtype)
    if router:
        l_ref = refs[pos + 1]
        l_ref[...] = jnp.dot(y, wr_ref[...], preferred_element_type=F32, precision=HIGHEST) + br_ref[...]


def _normmod(x, g, mods, *, shift_row=0, scale_row=1, out_dtype=F32, router=None, slab=False, n_ctx_rows=0,
             lat_seq=1):
    t, d = x.shape
    tm = TOKEN_TILE
    modulate = mods is not None
    group = functools.partial(_group_of, tile_rows=tm, n_ctx_rows=n_ctx_rows, lat_seq=lat_seq)
    in_specs = [pl.BlockSpec((tm, d), lambda i: (i, 0)), pl.BlockSpec((1, d), lambda i: (0, 0))]
    args = [x, g.reshape(1, d)]
    if modulate:
        in_specs.append(pl.BlockSpec((None, MOD_ROWS, d), lambda i: (group(i), 0, 0)))
        args.append(mods)
    out_specs = pl.BlockSpec((tm, d), lambda i: (i, 0))
    out_shape = jax.ShapeDtypeStruct((t, d), out_dtype)
    if slab:
        out_specs = pl.BlockSpec((tm * (d // LANES), LANES), lambda i: (i, 0))
        out_shape = jax.ShapeDtypeStruct((t * (d // LANES), LANES), F32)
    if router is not None:
        wr, br = router
        in_specs += [pl.BlockSpec((d, LANES), lambda i: (0, 0)), pl.BlockSpec((1, LANES), lambda i: (0, 0))]
        args += [wr, br]
        out_specs = (out_specs, pl.BlockSpec((tm, LANES), lambda i: (i, 0)))
        out_shape = (out_shape, jax.ShapeDtypeStruct((t, LANES), F32))
    return pl.pallas_call(
        functools.partial(_normmod_kernel, shift_row=shift_row, scale_row=scale_row,
                          modulate=modulate, router=router is not None, slab=slab),
        grid=(t // tm,),
        in_specs=in_specs,
        out_specs=out_specs,
        out_shape=out_shape,
        compiler_params=_params("parallel"),
        name="normmod",
    )(*args)


def _mm_kernel(*refs, out_act, has_bias, has_res, gate_row):
    a_ref, w_ref = refs[0], refs[1]
    pos = 2
    if has_bias:
        b_ref = refs[pos]
        pos += 1
    if has_res:
        x_ref, mod_ref = refs[pos], refs[pos + 1]
        pos += 2
    o_ref, wbf_ref = refs[pos], refs[pos + 1]

    @pl.when(pl.program_id(1) == 0)
    def _():
        wbf_ref[...] = w_ref[...].astype(BF16)

    acc = jnp.dot(a_ref[...].astype(BF16), wbf_ref[...], preferred_element_type=F32)
    if has_bias:
        acc = acc + b_ref[...]
    if out_act == "tanh":
        acc = jnp.tanh(acc)
    elif out_act == "sigmoid":
        acc = jax.nn.sigmoid(acc)
    if has_res:
        acc = x_ref[...] + mod_ref[gate_row:gate_row + 1, :] * acc
    o_ref[...] = acc.astype(o_ref.dtype)


def _matmul(a, w, bias=None, *, out_act=None, out_dtype=F32, res=None, tm=1024, tn=512):
    m, k = a.shape
    n = w.shape[1]
    tn = min(tn, n)
    tm = min(tm, m)
    in_specs = [pl.BlockSpec((tm, k), lambda j, i: (i, 0)), pl.BlockSpec((k, tn), lambda j, i: (0, j))]
    args = [a, w]
    if bias is not None:
        in_specs.append(pl.BlockSpec((1, tn), lambda j, i: (0, j)))
        args.append(bias.reshape(1, n))
    gate_row = 0
    if res is not None:
        x, mods, gate_row, n_ctx_rows, lat_seq = res
        group = functools.partial(_group_of, tile_rows=tm, n_ctx_rows=n_ctx_rows, lat_seq=lat_seq)
        in_specs += [pl.BlockSpec((tm, tn), lambda j, i: (i, j)),
                     pl.BlockSpec((None, MOD_ROWS, tn), lambda j, i: (group(i), 0, j))]
        args += [x, mods]
    return pl.pallas_call(
        functools.partial(_mm_kernel, out_act=out_act, has_bias=bias is not None,
                          has_res=res is not None, gate_row=gate_row),
        grid=(n // tn, m // tm),
        in_specs=in_specs,
        out_specs=pl.BlockSpec((tm, tn), lambda j, i: (i, j)),
        out_shape=jax.ShapeDtypeStruct((m, n), out_dtype),
        scratch_shapes=[pltpu.VMEM((k, tn), BF16)],
        compiler_params=_params("parallel", "arbitrary"),
        name="matmul",
    )(*args)


def _softplus(z):
    return jnp.maximum(z, 0.0) + jnp.log1p(jnp.exp(-jnp.abs(z)))


def _neg_expm1(z):
    t = jnp.tanh(0.5 * z)
    return -2.0 * t / (1.0 - t)


def _lru_kernel(gate_ref, xin_ref, cw_ref, cb_ref, wa_ref, ba_ref, wx_ref, bx_ref, lam_ref,
                h0f_ref, h0b_ref, y_ref, hf_ref, hb_ref, a_sc, u_sc, hs_sc):
    seq = xin_ref.shape[0]
    x = xin_ref[...]
    t_idx = lax.broadcasted_iota(I32, x.shape, 0)

    def shifted(off):
        if off == 0:
            return x
        rolled = pltpu.roll(x, (-off) % seq, 0)
        valid = (t_idx + off >= 0) & (t_idx + off < seq)
        return jnp.where(valid, rolled, 0.0)

    xc = cb_ref[...]
    for j in range(CONV_W):
        xc = xc + cw_ref[j:j + 1, :] * shifted(j - CONV_W // 2)
    xcb = xc.astype(BF16)
    for d in range(2):
        r = jax.nn.sigmoid(jnp.dot(xcb, wa_ref[d].astype(BF16), preferred_element_type=F32)
                           + ba_ref[d:d + 1, :])
        i = jax.nn.sigmoid(jnp.dot(xcb, wx_ref[d].astype(BF16), preferred_element_type=F32)
                           + bx_ref[d:d + 1, :])
        log_a = -LRU_C * r * _softplus(-lam_ref[d:d + 1, :])
        a_sc[d] = jnp.exp(log_a)
        u_sc[d] = jnp.sqrt(_neg_expm1(2.0 * log_a)) * (i * xc)

    def step(t, carry):
        hf, hb = carry
        tb = seq - 1 - t
        hf = a_sc[0, pl.ds(t, 1), :] * hf + u_sc[0, pl.ds(t, 1), :]
        hb = a_sc[1, pl.ds(tb, 1), :] * hb + u_sc[1, pl.ds(tb, 1), :]
        hs_sc[0, pl.ds(t, 1), :] = hf
        hs_sc[1, pl.ds(tb, 1), :] = hb
        return hf, hb

    hf, hb = lax.fori_loop(0, seq, step, (h0f_ref[...], h0b_ref[...]))
    hf_ref[...] = hf
    hb_ref[...] = hb
    y_ref[...] = ((hs_sc[0] + hs_sc[1]) * jax.nn.gelu(gate_ref[...])).astype(y_ref.dtype)


def _lru_scan(gx, n_seq, seq, row_off, conv_w, conv_b, w_a, b_a, w_x, b_x, lam, h0f, h0b):
    d = gx.shape[1] // 2
    c = d // LRU_BLOCKS
    seq_spec = lambda col_off: pl.BlockSpec((seq, c), lambda b, n: (b + row_off, n + col_off))
    vec2 = pl.BlockSpec((2, c), lambda b, n: (0, n))
    state = pl.BlockSpec((None, 1, c), lambda b, n: (b, 0, n))
    gate_w = pl.BlockSpec((2, None, c, c), lambda b, n: (0, n, 0, 0))
    return pl.pallas_call(
        _lru_kernel,
        grid=(n_seq, LRU_BLOCKS),
        in_specs=[seq_spec(0), seq_spec(LRU_BLOCKS),
                  pl.BlockSpec((CONV_W, c), lambda b, n: (0, n)),
                  pl.BlockSpec((1, c), lambda b, n: (0, n)),
                  gate_w, vec2, gate_w, vec2, vec2, state, state],
        out_specs=(pl.BlockSpec((seq, c), lambda b, n: (b, n)), state, state),
        out_shape=(jax.ShapeDtypeStruct((n_seq * seq, d), BF16),
                   jax.ShapeDtypeStruct((n_seq, 1, d), F32),
                   jax.ShapeDtypeStruct((n_seq, 1, d), F32)),
        scratch_shapes=[pltpu.VMEM((2, seq, c), F32)] * 3,
        compiler_params=_params("parallel", "parallel"),
        name="rglru",
    )(gx, gx, conv_w, conv_b.reshape(1, d), w_a, b_a, w_x, b_x, lam, h0f, h0b)


POOL_PAD = max(POOL_WINDOWS) // 2


def _pool_kernel(h_ref, w_ref, b_ref, sc_ref, x_ref, mod_ref, o_ref, xe_ref, *, gate_row):
    seq, c = h_ref.shape
    h = h_ref[...]
    zeros = jnp.zeros((POOL_PAD, c), F32)
    xe_ref[0:POOL_PAD, :] = zeros
    xe_ref[POOL_PAD + seq:POOL_PAD + seq + POOL_PAD, :] = zeros
    xe_ref[POOL_PAD:POOL_PAD + seq, :] = h
    t = lax.broadcasted_iota(I32, (seq, c), 0)
    grp = pl.program_id(1)
    for k, win in enumerate(POOL_WINDOWS):
        @pl.when(grp == k)
        def _(win=win):
            left = win // 2
            right = win - 1 - left
            total = xe_ref[POOL_PAD - left:POOL_PAD - left + seq, :]
            for off in range(-left + 1, right + 1):
                total = total + xe_ref[POOL_PAD + off:POOL_PAD + off + seq, :]
            count = (jnp.minimum(t + right, seq - 1) - jnp.maximum(t - left, 0) + 1).astype(F32)
            dpool = (total / count - h).astype(BF16)
            y = jnp.dot(dpool, w_ref[...].astype(BF16), preferred_element_type=F32) + b_ref[...]
            o_ref[...] = x_ref[...] + mod_ref[gate_row:gate_row + 1, :] * (y * sc_ref[...])


def _pool_mixer(h, x, mods, gate_row, w, b, scale, n_seq, seq, row_off):
    d = h.shape[1]
    ng = len(POOL_WINDOWS)
    c = d // ng
    seq_spec = pl.BlockSpec((seq, c), lambda s, g: (s + row_off, g))
    return pl.pallas_call(
        functools.partial(_pool_kernel, gate_row=gate_row),
        grid=(n_seq, ng),
        in_specs=[seq_spec,
                  pl.BlockSpec((None, c, c), lambda s, g: (g, 0, 0)),
                  pl.BlockSpec((None, 1, c), lambda s, g: (g, 0, 0)),
                  pl.BlockSpec((1, c), lambda s, g: (0, g)),
                  seq_spec,
                  pl.BlockSpec((None, MOD_ROWS, c), lambda s, g: (0 if row_off == 0 else 1 + s, 0, g))],
        out_specs=pl.BlockSpec((seq, c), lambda s, g: (s, g)),
        out_shape=jax.ShapeDtypeStruct((n_seq * seq, d), F32),
        scratch_shapes=[pltpu.VMEM((seq + 2 * POOL_PAD, c), F32)],
        compiler_params=_params("parallel", "parallel"),
        name="pool_mixer",
    )(h, w, b.reshape(ng, 1, c), scale.reshape(1, d), x, mods)


def _shiftmix_kernel(h_ref, mix_ref, o_ref):
    seq = h_ref.shape[0]
    h = h_ref[...]
    t = lax.broadcasted_iota(I32, h.shape, 0)
    prev = jnp.where(t >= 1, pltpu.roll(h, 1, 0), 0.0)
    nxt = jnp.where(t < seq - 1, pltpu.roll(h, seq - 1, 0), 0.0)
    xx = 0.5 * (prev + nxt) - h
    for n in range(N_TOKEN_MIX):
        o_ref[n] = (h + xx * mix_ref[n:n + 1, :]).astype(o_ref.dtype)


def _shiftmix(h, mix, n_seq, seq, row_off, c=512):
    t, d = h.shape
    return pl.pallas_call(
        _shiftmix_kernel,
        grid=(n_seq, d // c),
        in_specs=[pl.BlockSpec((seq, c), lambda s, j: (s + row_off, j)),
                  pl.BlockSpec((N_TOKEN_MIX, c), lambda s, j: (0, j))],
        out_specs=pl.BlockSpec((N_TOKEN_MIX, seq, c), lambda s, j: (0, s, j)),
        out_shape=jax.ShapeDtypeStruct((N_TOKEN_MIX, n_seq * seq, d), BF16),
        compiler_params=_params("parallel", "parallel"),
        name="rwkv_shiftmix",
    )(h, mix)


def _head_sums(x):
    r = lax.broadcasted_iota(I32, (LANES, LANES), 0) // RWKV_HEAD
    c = lax.broadcasted_iota(I32, (LANES, LANES), 1) // RWKV_HEAD
    ones = jnp.where(r == c, 1.0, 0.0).astype(BF16)
    outs = []
    for j in range(x.shape[1] // LANES):
        xs = x[:, j * LANES:(j + 1) * LANES]
        hi = xs.astype(BF16)
        rem = xs - hi.astype(F32)
        mid = rem.astype(BF16)
        lo = (rem - mid.astype(F32)).astype(BF16)
        outs.append(jnp.dot(hi, ones, preferred_element_type=F32)
                    + jnp.dot(mid, ones, preferred_element_type=F32)
                    + jnp.dot(lo, ones, preferred_element_type=F32))
    return jnp.concatenate(outs, axis=-1)


def _rwkv_prep_kernel(r_ref, k_ref, v_ref, dec_ref, aa_ref, kk_ref, ka_ref, rk_ref,
                      a_ref, w_ref, kd_ref, b_ref, bonus_ref):
    d = r_ref.shape[1]
    r = r_ref[...]
    k = k_ref[...]
    kk = k * kk_ref[...]
    kk = kk / jnp.maximum(jnp.sqrt(_head_sums(kk * kk)), 1e-12)
    a_ref[...] = -kk
    for dr in range(2):
        w_log = -_softplus(-dec_ref[:, dr * d:(dr + 1) * d]) - 0.5
        w_ref[dr] = jnp.exp(-jnp.exp(w_log))
        iclr = jax.nn.sigmoid(aa_ref[:, dr * d:(dr + 1) * d])
        kd_ref[dr] = k * (1.0 + (iclr - 1.0) * ka_ref[...])
        b_ref[dr] = kk * iclr
    bonus_ref[...] = _head_sums(r * k * rk_ref[...]) * v_ref[...]


def _rwkv_prep(r, k, v, dec, aa, k_k, k_a, r_k):
    t, d = r.shape
    tm = TOKEN_TILE // 2
    row = pl.BlockSpec((tm, d), lambda i: (i, 0))
    row2 = pl.BlockSpec((tm, 2 * d), lambda i: (i, 0))
    vec = pl.BlockSpec((1, d), lambda i: (0, 0))
    both = pl.BlockSpec((2, tm, d), lambda i: (0, i, 0))
    one = jax.ShapeDtypeStruct((t, d), F32)
    two = jax.ShapeDtypeStruct((2, t, d), F32)
    return pl.pallas_call(
        _rwkv_prep_kernel,
        grid=(t // tm,),
        in_specs=[row, row, row, row2, row2, vec, vec, vec],
        out_specs=(row, both, both, both, row),
        out_shape=(one, two, two, two, one),
        compiler_params=_params("parallel"),
        name="rwkv_prep",
    )(r, k, v, dec, aa, k_k.reshape(1, d), k_a.reshape(1, d), r_k.reshape(1, d))


SCAN_VB = 8


def _rwkv_scan_kernel(r_ref, w_ref, k_ref, a_ref, b_ref, v_ref, s0_ref, y_ref, st_ref, state):
    chunk = pl.program_id(1)
    n_t, n_k = r_ref.shape[0], r_ref.shape[1]
    n_vb = state.shape[0]

    @pl.when(chunk == 0)
    def _():
        state[...] = s0_ref[...]

    def tree_sum(parts):
        while len(parts) > 1:
            parts = [parts[i] + parts[i + 1] for i in range(0, len(parts), 2)]
        return parts[0]

    def t_step(t, _):
        def vb_step(vb, _):
            parts = [None] * 4
            for kk in range(n_k):
                term = state[vb, kk] * a_ref[t, kk:kk + 1, :]
                parts[kk % 4] = term if parts[kk % 4] is None else parts[kk % 4] + term
            sa = tree_sum(parts)
            rows = pl.ds(pl.multiple_of(vb * SCAN_VB, SCAN_VB), SCAN_VB)
            vv = v_ref[t, rows, :]
            parts = [None] * 4
            for kk in range(n_k):
                s = (state[vb, kk] * w_ref[t, kk:kk + 1, :] + sa * b_ref[t, kk:kk + 1, :]
                     + vv * k_ref[t, kk:kk + 1, :])
                state[vb, kk] = s
                term = s * r_ref[t, kk:kk + 1, :]
                parts[kk % 4] = term if parts[kk % 4] is None else parts[kk % 4] + term
            y_ref[t, rows, :] = tree_sum(parts)
            return 0

        lax.fori_loop(0, n_vb, vb_step, 0)
        return 0

    lax.fori_loop(0, n_t, t_step, 0)

    @pl.when(chunk == pl.num_programs(1) - 1)
    def _():
        st_ref[...] = state[...]


def _rwkv_scan(r, w, k, a, b, v, s0):
    seq, n, chains = r.shape
    tc = min(SCAN_CHUNK, seq)
    seq_spec = pl.BlockSpec((tc, n, LANES), lambda g, c: (c, 0, g))
    st_spec = pl.BlockSpec((n // SCAN_VB, n, SCAN_VB, LANES), lambda g, c: (0, 0, 0, g))
    return pl.pallas_call(
        _rwkv_scan_kernel,
        grid=(chains // LANES, seq // tc),
        in_specs=[seq_spec] * 6 + [st_spec],
        out_specs=(seq_spec, st_spec),
        out_shape=(jax.ShapeDtypeStruct((seq, n, chains), F32),
                   jax.ShapeDtypeStruct((n // SCAN_VB, n, SCAN_VB, chains), F32)),
        scratch_shapes=[pltpu.VMEM((n // SCAN_VB, n, SCAN_VB, LANES), F32)],
        compiler_params=_params("parallel", "arbitrary"),
        name="rwkv_scan",
    )(r, w, k, a, b, v, s0)


def _rwkv_post_kernel(yf_ref, yb_ref, bonus_ref, g_ref, lnw_ref, lnb_ref, o_ref):
    y = yf_ref[...] + yb_ref[...]
    mu = _head_sums(y) * (1.0 / RWKV_HEAD)
    yc = y - mu
    var = _head_sums(yc * yc) * (1.0 / RWKV_HEAD)
    y = yc * lax.rsqrt(var + RWKV_GN_EPS) * lnw_ref[...] + lnb_ref[...]
    o_ref[...] = ((y + bonus_ref[...]) * g_ref[...]).astype(o_ref.dtype)


def _rwkv_post(yf, yb, bonus, g, ln_w, ln_b):
    t, d = yf.shape
    tm = TOKEN_TILE
    row = pl.BlockSpec((tm, d), lambda i: (i, 0))
    vec = pl.BlockSpec((1, d), lambda i: (0, 0))
    return pl.pallas_call(
        _rwkv_post_kernel,
        grid=(t // tm,),
        in_specs=[row, row, row, row, vec, vec],
        out_specs=row,
        out_shape=jax.ShapeDtypeStruct((t, d), BF16),
        compiler_params=_params("parallel"),
        name="rwkv_post",
    )(yf, yb, bonus, g, ln_w.reshape(1, d), ln_b.reshape(1, d))


def _rope_kernel(x_ref, cos_ref, sin_ref, o_ref):
    x = x_ref[...]
    hd = x.shape[1]
    quarter = hd // 4
    src = lax.broadcasted_iota(I32, (hd, hd), 0)
    dst = lax.broadcasted_iota(I32, (hd, hd), 1)
    partner = jnp.where((dst // quarter) % 2 == 0, dst + quarter, dst - quarter)
    perm = jnp.where(src == partner, 1.0, 0.0).astype(BF16)
    hi = x.astype(BF16)
    rem = x - hi.astype(F32)
    mid = rem.astype(BF16)
    lo = (rem - mid.astype(F32)).astype(BF16)
    swapped = (jnp.dot(hi, perm, preferred_element_type=F32)
               + jnp.dot(mid, perm, preferred_element_type=F32)
               + jnp.dot(lo, perm, preferred_element_type=F32))
    o_ref[...] = (x * cos_ref[...] + swapped * sin_ref[...]).astype(o_ref.dtype)


def _rope(x, cos, sin):
    n, seq, hd = x.shape
    tab = pl.BlockSpec((seq, hd), lambda i: (0, 0))
    return pl.pallas_call(
        _rope_kernel,
        grid=(n,),
        in_specs=[pl.BlockSpec((None, seq, hd), lambda i: (i, 0, 0)), tab, tab],
        out_specs=pl.BlockSpec((None, seq, hd), lambda i: (i, 0, 0)),
        out_shape=jax.ShapeDtypeStruct((n, seq, hd), BF16),
        compiler_params=_params("parallel"),
        name="rope",
    )(x, cos, sin)


def _attn_kernel(sink_ref, q_ref, k_ref, v_ref, o_ref, *, n_cache, window):
    tq = q_ref.shape[0]
    n_k = k_ref.shape[0]
    sink = sink_ref[pl.program_id(1) * KV_GROUP + pl.program_id(2)]
    s = lax.dot_general(q_ref[...].astype(BF16), k_ref[...].astype(BF16),
                        (((1,), (1,)), ((), ())), preferred_element_type=F32) * ATTN_SCALE
    if window is not None:
        q_pos = pl.program_id(3) * tq + lax.broadcasted_iota(I32, (tq, n_k), 0)
        col = lax.broadcasted_iota(I32, (tq, n_k), 1)
        valid = (col < n_cache) | (jnp.abs(q_pos - (col - n_cache)) <= window)
        s = jnp.where(valid, s, NEG_INF)
    m = jnp.maximum(jnp.max(s, axis=-1, keepdims=True), sink)
    p = jnp.exp(s - m)
    denom = jnp.sum(p, axis=-1, keepdims=True) + jnp.exp(sink - m)
    o = jnp.dot(p.astype(BF16), v_ref[...].astype(BF16), preferred_element_type=F32)
    o_ref[...] = (o / denom).astype(o_ref.dtype)


def _attention(q, k, v, sinks, *, n_cache=0, window=None, tq=256):
    b, kv, g, sq, hd = q.shape
    sk = k.shape[2]
    kv_spec = pl.BlockSpec((None, None, sk, hd), lambda bi, ki, gi, qi, s: (bi, ki, 0, 0))
    q_spec = pl.BlockSpec((None, None, None, tq, hd), lambda bi, ki, gi, qi, s: (bi, ki, gi, qi, 0))
    return pl.pallas_call(
        functools.partial(_attn_kernel, n_cache=n_cache, window=window),
        grid_spec=pltpu.PrefetchScalarGridSpec(
            num_scalar_prefetch=1,
            grid=(b, kv, g, sq // tq),
            in_specs=[q_spec, kv_spec, kv_spec],
            out_specs=q_spec),
        out_shape=jax.ShapeDtypeStruct(q.shape, BF16),
        compiler_params=_params("parallel", "parallel", "parallel", "parallel"),
        name="attention",
    )(sinks, q, k, v)


def _rope_tables(seq):
    quarter = HEAD_DIM // 4
    pos = jnp.arange(seq)
    rows = (pos // GRID_W).astype(F32)
    cols = (pos % GRID_W).astype(F32)
    inv_freq = ROPE_BASE ** (-jnp.arange(quarter, dtype=F32) / quarter)
    ang_r = rows[:, None] * inv_freq
    ang_c = cols[:, None] * inv_freq
    cos = jnp.concatenate([jnp.cos(ang_r)] * 2 + [jnp.cos(ang_c)] * 2, axis=1)
    sin = jnp.concatenate([-jnp.sin(ang_r), jnp.sin(ang_r), -jnp.sin(ang_c), jnp.sin(ang_c)], axis=1)
    return cos, sin


def _topk_kernel(l_ref, e_ref, g_ref):
    x = l_ref[...]
    lane = lax.broadcasted_iota(I32, x.shape, 1)
    lane_f = lane.astype(F32)
    x = jnp.where(lane < N_EXPERTS, x, -jnp.inf)
    e_out = jnp.zeros(x.shape, F32)
    tops = []
    for r in range(TOP_K):
        m = jnp.max(x, axis=-1, keepdims=True)
        idx = jnp.min(jnp.where(x == m, lane_f, float(LANES)), axis=-1, keepdims=True)
        e_out = jnp.where(lane == r, idx, e_out)
        x = jnp.where(lane_f == idx, -jnp.inf, x)
        tops.append(m)
    ex = [jnp.exp(m - tops[0]) for m in tops]
    tot = ex[0] + ex[1] + ex[2] + ex[3]
    g_out = jnp.zeros(x.shape, F32)
    for r in range(TOP_K):
        g_out = jnp.where(lane == r, ex[r] / tot, g_out)
    e_ref[...] = e_out.astype(I32)
    g_ref[...] = g_out


def _topk(logits):
    t = logits.shape[0]
    tm = TOKEN_TILE
    spec = pl.BlockSpec((tm, LANES), lambda i: (i, 0))
    return pl.pallas_call(
        _topk_kernel,
        grid=(t // tm,),
        in_specs=[spec],
        out_specs=(spec, spec),
        out_shape=(jax.ShapeDtypeStruct((t, LANES), I32), jax.ShapeDtypeStruct((t, LANES), F32)),
        compiler_params=_params("parallel"),
        name="router_topk",
    )(logits)


def _rank_kernel(e_ref, rank_ref, cnt_ref, carry):
    @pl.when(pl.program_id(0) == 0)
    def _():
        carry[...] = jnp.zeros(carry.shape, F32)

    e = e_ref[...]
    tm = e.shape[0]
    lane = lax.broadcasted_iota(I32, e.shape, 1)
    hits = []
    member = jnp.zeros(e.shape, F32)
    for r in range(TOP_K):
        hit = lane == e[:, r:r + 1]
        hits.append(hit)
        member = member + jnp.where(hit, 1.0, 0.0)
    row = lax.broadcasted_iota(I32, (tm, tm), 0)
    col = lax.broadcasted_iota(I32, (tm, tm), 1)
    tri = jnp.where(row > col, 1.0, 0.0).astype(BF16)
    before = jnp.dot(tri, member.astype(BF16), preferred_element_type=F32) + carry[0:1, :]
    out = jnp.zeros(e.shape, F32)
    for r in range(TOP_K):
        rk = jnp.sum(jnp.where(hits[r], before, 0.0), axis=-1, keepdims=True)
        out = jnp.where(lane == r, rk, out)
    rank_ref[...] = out.astype(I32)
    carry[...] = carry[...] + jnp.sum(member, axis=0, keepdims=True)
    cnt_ref[...] = carry[...]


def _rank(top_e):
    t = top_e.shape[0]
    tm = TOKEN_TILE
    spec = pl.BlockSpec((tm, LANES), lambda i: (i, 0))
    cnt_spec = pl.BlockSpec((8, LANES), lambda i: (0, 0))
    return pl.pallas_call(
        _rank_kernel,
        grid=(t // tm,),
        in_specs=[spec],
        out_specs=(spec, cnt_spec),
        out_shape=(jax.ShapeDtypeStruct((t, LANES), I32), jax.ShapeDtypeStruct((8, LANES), F32)),
        scratch_shapes=[pltpu.VMEM((8, LANES), F32)],
        compiler_params=_params("arbitrary"),
        name="router_rank",
    )(top_e)


def _slab_copy(src_hbm, row, buf, slot, sem):
    n_chunks = src_hbm.shape[1]
    return pltpu.make_async_copy(src_hbm.at[row], buf.at[pl.ds(slot * n_chunks, n_chunks), :], sem)


def _gather_kernel(idx_ref, src_hbm, o_ref, buf, sem):
    n = o_ref.shape[0]
    n_chunks = src_hbm.shape[1]
    base = pl.program_id(0) * n

    def issue(r, _):
        _slab_copy(src_hbm, idx_ref[base + r], buf, r, sem).start()
        return 0

    lax.fori_loop(0, n, issue, 0, unroll=8)

    def drain(r, _):
        _slab_copy(src_hbm, 0, buf, r, sem).wait()
        return 0

    lax.fori_loop(0, n, drain, 0, unroll=8)
    for c in range(n_chunks):
        o_ref[:, c * LANES:(c + 1) * LANES] = buf[pl.ds(c, n, stride=n_chunks), :].astype(o_ref.dtype)


def _gather_rows(src, idx, out_dtype):
    n = idx.shape[0]
    n_chunks = src.shape[1]
    tm = MOE_TILE
    return pl.pallas_call(
        _gather_kernel,
        grid_spec=pltpu.PrefetchScalarGridSpec(
            num_scalar_prefetch=1,
            grid=(n // tm,),
            in_specs=[pl.BlockSpec(memory_space=pl.ANY)],
            out_specs=pl.BlockSpec((tm, n_chunks * LANES), lambda i, idx: (i, 0)),
            scratch_shapes=[pltpu.VMEM((tm * n_chunks, LANES), src.dtype), pltpu.SemaphoreType.DMA(())]),
        out_shape=jax.ShapeDtypeStruct((n, n_chunks * LANES), out_dtype),
        compiler_params=_params("arbitrary"),
        name="moe_gather",
    )(idx, src)


def _gmm_kernel(exp_ref, first_ref, slot_ref, nxt_e_ref, nxt_j_ref, valid_ref, blk_ref, *refs,
                swiglu, layer, tn, n_blocks, slab_out):
    if swiglu:
        x_ref, w_hbm, bg_ref, bu_ref, o_ref, wbuf, wbf, sem = refs
    else:
        x_ref, w_hbm, bg_ref, o_ref, wbuf, wbf, sem = refs
    n_w = 2 if swiglu else 1
    n_half = w_hbm.shape[3] // n_w
    s = pl.program_id(0)

    def w_copy(e, j, slot, which):
        col = pl.multiple_of(j * tn + which * n_half, LANES)
        return pltpu.make_async_copy(w_hbm.at[layer, e, :, pl.ds(col, tn)], wbuf.at[slot, which],
                                     sem.at[slot, which])

    @pl.when(valid_ref[s] == 1)
    def _():
        @pl.when(first_ref[s] == 1)
        def _():
            slot = slot_ref[s]

            @pl.when(s == 0)
            def _():
                for which in range(n_w):
                    w_copy(exp_ref[s], 0, slot, which).start()

            for which in range(n_w):
                w_copy(exp_ref[s], s // n_blocks, slot, which).wait()
                wbf[which] = wbuf[slot, which].astype(BF16)

            @pl.when(nxt_e_ref[s] >= 0)
            def _():
                for which in range(n_w):
                    w_copy(nxt_e_ref[s], nxt_j_ref[s], 1 - slot, which).start()

        x = x_ref[...]
        g = jnp.dot(x, wbf[0], preferred_element_type=F32) + bg_ref[...]
        if swiglu:
            u = jnp.dot(x, wbf[1], preferred_element_type=F32) + bu_ref[...]
            g = jnp.minimum(g, SWIGLU_LIMIT)
            u = jnp.clip(u, -SWIGLU_LIMIT, SWIGLU_LIMIT)
            g = (u + 1.0) * (g * jax.nn.sigmoid(SWIGLU_ALPHA * g))
        if slab_out:
            n_chunks = tn // LANES
            for c in range(n_chunks):
                o_ref[pl.ds(c, g.shape[0], stride=n_chunks), :] = g[:, c * LANES:(c + 1) * LANES]
        else:
            o_ref[...] = g.astype(o_ref.dtype)

    @pl.when(valid_ref[s] == 0)
    def _():
        o_ref[...] = jnp.zeros(o_ref.shape, o_ref.dtype)


def _grouped_matmul(x, w, b, layer, sched, *, swiglu, out_dtype, tn, slab_out=False):
    p, k = x.shape
    n_l, n_e = w.shape[:2]
    n_w = 2 if swiglu else 1
    n = w.shape[3] // n_w
    tm = MOE_TILE
    nj = n // tn
    n_blocks = p // tm
    assert not slab_out or nj == 1
    b4 = b.reshape(n_l, n_e, 1, b.shape[2])
    x_spec = pl.BlockSpec((tm, k), lambda s, ex, fi, sl, ne, njx, va, bl: (bl[s], 0))
    b_spec = lambda off: pl.BlockSpec((None, None, 1, tn),
                                      lambda s, ex, fi, sl, ne, njx, va, bl: (layer, ex[s], 0, s // n_blocks + off))
    in_specs = [x_spec, pl.BlockSpec(memory_space=pl.ANY), b_spec(0)]
    args = [x, w, b4]
    if swiglu:
        in_specs.append(b_spec(nj))
        args.append(b4)
    if slab_out:
        out_spec = pl.BlockSpec((tm * (n // LANES), LANES), lambda s, ex, fi, sl, ne, njx, va, bl: (s, 0))
        out_shape = jax.ShapeDtypeStruct((p * (n // LANES), LANES), out_dtype)
    else:
        out_spec = pl.BlockSpec((tm, tn), lambda s, ex, fi, sl, ne, njx, va, bl: (s % n_blocks, s // n_blocks))
        out_shape = jax.ShapeDtypeStruct((p, n), out_dtype)
    return pl.pallas_call(
        functools.partial(_gmm_kernel, swiglu=swiglu, layer=layer, tn=tn, n_blocks=n_blocks, slab_out=slab_out),
        grid_spec=pltpu.PrefetchScalarGridSpec(
            num_scalar_prefetch=7,
            grid=(nj * n_blocks,),
            in_specs=in_specs,
            out_specs=out_spec,
            scratch_shapes=[pltpu.VMEM((2, n_w, k, tn), F32), pltpu.VMEM((n_w, k, tn), BF16),
                            pltpu.SemaphoreType.DMA((2, n_w))]),
        out_shape=out_shape,
        compiler_params=_params("arbitrary"),
        name="moe_experts",
    )(*sched(nj), *args)


def _moe_schedule(blocks_per_e, n_blocks):
    blk_end = jnp.cumsum(blocks_per_e)
    n_used = blk_end[-1]
    i = jnp.arange(n_blocks, dtype=I32)
    blk_i = jnp.minimum(i, n_used - 1)
    exp_i = jnp.sum((blk_end[None, :] <= blk_i[:, None]).astype(I32), axis=1)
    valid_i = i < n_used
    first_i = valid_i & jnp.concatenate([jnp.ones((1,), bool), exp_i[1:] != exp_i[:-1]])

    def sched(nj):
        n_steps = nj * n_blocks
        exp = jnp.tile(exp_i, nj)
        first = jnp.tile(first_i, nj)
        step = jnp.arange(n_steps, dtype=I32)
        slot = (jnp.cumsum(first.astype(I32)) - 1) % 2
        pos = jnp.where(first, step, n_steps)
        nxt = jnp.concatenate([lax.cummin(pos[::-1])[::-1][1:], jnp.full((1,), n_steps, I32)])
        has_nxt = nxt < n_steps
        nxt_c = jnp.minimum(nxt, n_steps - 1)
        nxt_e = jnp.where(has_nxt, exp[nxt_c], -1)
        nxt_j = nxt_c // n_blocks
        return (exp, first.astype(I32), slot.astype(I32), nxt_e.astype(I32), nxt_j.astype(I32),
                jnp.tile(valid_i, nj).astype(I32), jnp.tile(blk_i, nj))

    return sched


def _combine_kernel(dest_ref, ys_hbm, gate_ref, x_ref, mod_ref, o_ref, buf, sem, *, gate_row):
    tm = x_ref.shape[0]
    n = tm * TOP_K
    n_chunks = ys_hbm.shape[1]
    base = pl.program_id(0) * n

    def issue(r, _):
        _slab_copy(ys_hbm, dest_ref[base + r], buf, r, sem).start()
        return 0

    lax.fori_loop(0, n, issue, 0, unroll=8)

    def drain(r, _):
        _slab_copy(ys_hbm, 0, buf, r, sem).wait()
        return 0

    lax.fori_loop(0, n, drain, 0, unroll=8)
    gates = [jnp.broadcast_to(gate_ref[:, k:k + 1], (tm, LANES)) for k in range(TOP_K)]
    for c in range(n_chunks):
        acc = gates[0] * buf[pl.ds(c, tm, stride=n_chunks), :]
        for k in range(1, TOP_K):
            acc = acc + gates[k] * buf[pl.ds(k * tm * n_chunks + c, tm, stride=n_chunks), :]
        cols = slice(c * LANES, (c + 1) * LANES)
        o_ref[:, cols] = x_ref[:, cols] + mod_ref[gate_row:gate_row + 1, cols] * acc


def _combine(ys, dest_blocked, gates, x, mods, gate_row, n_ctx_rows, lat_seq):
    t, d = x.shape
    tm = COMBINE_TILE
    group = functools.partial(_group_of, tile_rows=tm, n_ctx_rows=n_ctx_rows, lat_seq=lat_seq)
    return pl.pallas_call(
        functools.partial(_combine_kernel, gate_row=gate_row),
        grid_spec=pltpu.PrefetchScalarGridSpec(
            num_scalar_prefetch=1,
            grid=(t // tm,),
            in_specs=[pl.BlockSpec(memory_space=pl.ANY),
                      pl.BlockSpec((tm, LANES), lambda i, de: (i, 0)),
                      pl.BlockSpec((tm, d), lambda i, de: (i, 0)),
                      pl.BlockSpec((None, MOD_ROWS, d), lambda i, de: (group(i), 0, 0))],
            out_specs=pl.BlockSpec((tm, d), lambda i, de: (i, 0)),
            scratch_shapes=[pltpu.VMEM((tm * TOP_K * (d // LANES), LANES), F32), pltpu.SemaphoreType.DMA(())]),
        out_shape=jax.ShapeDtypeStruct((t, d), F32),
        compiler_params=_params("arbitrary"),
        name="moe_combine",
    )(dest_blocked, ys, gates, x, mods)


def _moe_layer(x, g_norm, mods, layer, w_router, b_router, w_gu, b_gu, w_down, b_down, n_ctx_rows, lat_seq):
    t, d = x.shape
    n_e = w_router.shape[1]
    n_chunks = d // LANES
    wr = jnp.pad(w_router, ((0, 0), (0, LANES - n_e)))
    br = jnp.pad(b_router, (0, LANES - n_e)).reshape(1, LANES)
    h, logits = _normmod(x, g_norm, mods, shift_row=3, scale_row=4, router=(wr, br), slab=True,
                         n_ctx_rows=n_ctx_rows, lat_seq=lat_seq)
    top_e, gates = _topk(logits)
    rank, counts = _rank(top_e)

    counts = counts[0, :n_e].astype(I32)
    blocks_per_e = (counts + MOE_TILE - 1) // MOE_TILE
    seg_start = (jnp.cumsum(blocks_per_e) - blocks_per_e) * MOE_TILE
    n_blocks = -(-(t * TOP_K) // MOE_TILE) + n_e
    sched = _moe_schedule(blocks_per_e, n_blocks)
    dest = seg_start[top_e[:, :TOP_K]] + rank[:, :TOP_K]
    tok = jnp.broadcast_to(jnp.arange(t, dtype=I32)[:, None], (t, TOP_K))
    src_tok = jnp.zeros((n_blocks * MOE_TILE,), I32).at[dest.reshape(-1)].set(tok.reshape(-1))
    dest_blocked = dest.reshape(t // COMBINE_TILE, COMBINE_TILE, TOP_K).transpose(0, 2, 1).reshape(-1)

    xs = _gather_rows(h.reshape(t, n_chunks, LANES), src_tok, BF16)
    act = _grouped_matmul(xs, w_gu, b_gu, layer, sched, swiglu=True, out_dtype=BF16, tn=1024)
    ys = _grouped_matmul(act, w_down, b_down, layer, sched, swiglu=False, out_dtype=F32, tn=w_down.shape[3],
                         slab_out=True)
    return _combine(ys.reshape(n_blocks * MOE_TILE, n_chunks, LANES), dest_blocked, gates, x, mods, 5,
                    n_ctx_rows, lat_seq)


def _dims(dims):
    bp, sp, bs, ss = dims
    n_ctx = bp * sp
    return n_ctx, n_ctx // ss, dict(n_ctx_rows=n_ctx, lat_seq=ss)


def _lru_layer(x, mods, dims, g_norm, w_in, conv_w, conv_b, w_a, b_a, w_x, b_x, lam, w_out, h0f, h0b):
    bp, sp, bs, ss = dims
    n_ctx, lat_off, grp = _dims(dims)
    d = x.shape[1]
    h = _normmod(x, g_norm, mods, out_dtype=BF16, **grp)
    gx = _matmul(h, w_in)
    lru = (conv_w, conv_b, w_a, b_a, w_x, b_x, lam)
    zero = jnp.zeros((bp, 1, d), F32)
    y_c, hf, hb = _lru_scan(gx, bp, sp, 0, *lru, zero, zero)
    y_l, _, _ = _lru_scan(gx, bs, ss, lat_off, *lru, h0f, h0b)
    y = jnp.concatenate([y_c, y_l], axis=0)
    return _matmul(y, w_out, res=(x, mods, 2, n_ctx, ss)), hf, hb


def _pool_layer(x, mods, dims, g_norm, w, b, scale):
    bp, sp, bs, ss = dims
    n_ctx, lat_off, grp = _dims(dims)
    h = _normmod(x, g_norm, mods, **grp)
    x_c = _pool_mixer(h, x, mods, 2, w, b, scale, bp, sp, 0)
    x_l = _pool_mixer(h, x, mods, 2, w, b, scale, bs, ss, lat_off)
    return jnp.concatenate([x_c, x_l], axis=0)


def _rwkv_layer(x, mods, dims, g_norm, mix, w_r, w_k, w_v, w_o, dec_w0, dec_w1, dec_w2, a_w0, a_w1, a_w2,
                g_w1, g_w2, k_k, k_a, r_k, ln_w, ln_b, s0f, s0b):
    bp, sp, bs, ss = dims
    n_ctx, lat_off, grp = _dims(dims)
    d = x.shape[1]
    heads = d // RWKV_HEAD
    h = _normmod(x, g_norm, mods, **grp)
    x6 = jnp.concatenate([_shiftmix(h, mix, bp, sp, 0), _shiftmix(h, mix, bs, ss, lat_off)], axis=1)
    xr, xw, xk, xv, xa, xg = (x6[n] for n in range(N_TOKEN_MIX))
    r = _matmul(xr, w_r)
    k = _matmul(xk, w_k)
    v = _matmul(xv, w_v)
    pad = 2 * LANES - 2 * dec_w1.shape[-1]

    def lora_in(w1):
        return jnp.pad(jnp.concatenate([w1[0], w1[1]], axis=1), ((0, 0), (0, pad)))

    def lora_out(w2):
        z = jnp.zeros_like(w2[0])
        top = jnp.concatenate([w2[0], z], axis=1)
        bot = jnp.concatenate([z, w2[1]], axis=1)
        return jnp.pad(jnp.concatenate([top, bot], axis=0), ((0, pad), (0, 0)))

    dec_h = _matmul(xw, lora_in(dec_w1), out_act="tanh", out_dtype=BF16)
    dec = _matmul(dec_h, lora_out(dec_w2), dec_w0.reshape(-1))
    aa_h = _matmul(xa, lora_in(a_w1), out_dtype=BF16)
    aa = _matmul(aa_h, lora_out(a_w2), a_w0.reshape(-1))
    g_h = _matmul(xg, g_w1, out_act="sigmoid", out_dtype=BF16)
    g = _matmul(g_h, g_w2)
    a_neg, w_dec, k_d, b_d, bonus = _rwkv_prep(r, k, v, dec, aa, k_k, k_a, r_k.reshape(-1))

    def scan_group(lo, n_seq, seq, s0_fwd, s0_bwd):
        def lay(z, flip):
            z = z[lo:lo + n_seq * seq].reshape(n_seq, seq, heads, RWKV_HEAD)
            z = z.transpose(1, 3, 0, 2).reshape(seq, RWKV_HEAD, n_seq * heads)
            return z[::-1] if flip else z

        both = lambda zf, zb: jnp.concatenate([lay(zf, False), lay(zb, True)], axis=-1)
        def st(z):
            z = z.reshape(n_seq * heads, RWKV_HEAD // SCAN_VB, SCAN_VB, RWKV_HEAD)
            return z.transpose(1, 3, 2, 0)

        y, s_fin = _rwkv_scan(both(r, r), both(w_dec[0], w_dec[1]), both(k_d[0], k_d[1]),
                              both(a_neg, a_neg), both(b_d[0], b_d[1]), both(v, v),
                              jnp.concatenate([st(s0_fwd), st(s0_bwd)], axis=-1))
        nc = n_seq * heads

        def unlay(z):
            return z.reshape(seq, RWKV_HEAD, n_seq, heads).transpose(2, 0, 3, 1).reshape(n_seq * seq, d)

        def unst(z):
            return z.transpose(3, 0, 2, 1).reshape(n_seq, heads, RWKV_HEAD, RWKV_HEAD)

        return unlay(y[:, :, :nc]), unlay(y[::-1, :, nc:]), unst(s_fin[..., :nc]), unst(s_fin[..., nc:])

    zero = jnp.zeros((bp, heads, RWKV_HEAD, RWKV_HEAD), F32)
    yf_c, yb_c, sf, sb = scan_group(0, bp, sp, zero, zero)
    yf_l, yb_l, _, _ = scan_group(n_ctx, bs, ss, s0f, s0b)
    yf = jnp.concatenate([yf_c, yf_l], axis=0)
    yb = jnp.concatenate([yb_c, yb_l], axis=0)
    yg = _rwkv_post(yf, yb, bonus, g, ln_w, ln_b)
    return _matmul(yg, w_o, res=(x, mods, 2, n_ctx, ss)), sf, sb


def _attn_layer(x, mods, dims, g_norm, w_qkv, b_qkv, w_o, b_o, sinks, cache_k, cache_v):
    bp, sp, bs, ss = dims
    n_ctx, lat_off, grp = _dims(dims)
    h = _normmod(x, g_norm, mods, out_dtype=BF16, **grp)
    qkv = _matmul(h, w_qkv, b_qkv)
    nq = N_HEADS * HEAD_DIM
    nk = N_KV_HEADS * HEAD_DIM

    def split(lo, n_seq, seq):
        z = qkv[lo:lo + n_seq * seq]
        q = z[:, :nq].reshape(n_seq, seq, N_KV_HEADS, KV_GROUP, HEAD_DIM).transpose(0, 2, 3, 1, 4)
        kk = z[:, nq:nq + nk].reshape(n_seq, seq, N_KV_HEADS, HEAD_DIM)
        vv = z[:, nq + nk:].reshape(n_seq, seq, N_KV_HEADS, HEAD_DIM)
        return q, kk, vv

    def merge(o):
        n_seq, _, _, seq, _ = o.shape
        return o.transpose(0, 3, 1, 2, 4).reshape(n_seq * seq, nq)

    q_c, k_c, v_c = split(0, bp, sp)
    o_c = _attention(q_c, k_c.transpose(0, 2, 1, 3), v_c.transpose(0, 2, 1, 3), sinks)
    q_l, k_l, v_l = split(n_ctx, bs, ss)
    cos, sin = _rope_tables(ss)
    q_l = _rope(q_l.reshape(-1, ss, HEAD_DIM), cos, sin).reshape(q_l.shape)
    k_l = _rope(k_l.transpose(0, 2, 1, 3).reshape(-1, ss, HEAD_DIM), cos, sin)
    k_l = k_l.reshape(bs, N_KV_HEADS, ss, HEAD_DIM)
    k_all = jnp.concatenate([cache_k.transpose(0, 2, 1, 3).astype(BF16), k_l], axis=2)
    v_all = jnp.concatenate([cache_v.transpose(0, 2, 1, 3), v_l.transpose(0, 2, 1, 3)], axis=2)
    o_l = _attention(q_l, k_all, v_all, sinks, n_cache=cache_k.shape[1], window=WINDOW)
    o = jnp.concatenate([merge(o_c), merge(o_l)], axis=0)
    return _matmul(o, w_o, b_o, res=(x, mods, 2, n_ctx, ss)), k_c, v_c


def kernel(x_prompt, x_sample, c, c_ctx, state_lru_fwd, state_lru_bwd, state_rwkv_fwd, state_rwkv_bwd, cache_attn_k, cache_attn_v, norm_mix, norm_ffn, w_mod, b_mod, lru_w_in, lru_conv_w, lru_conv_b, lru_w_a, lru_b_a, lru_w_x, lru_b_x, lru_lambda, lru_w_out, pool_w, pool_b, pool_scale, rwkv_mix, rwkv_w_r, rwkv_w_k, rwkv_w_v, rwkv_w_o, rwkv_dec_w0, rwkv_dec_w1, rwkv_dec_w2, rwkv_a_w0, rwkv_a_w1, rwkv_a_w2, rwkv_g_w1, rwkv_g_w2, rwkv_k_k, rwkv_k_a, rwkv_r_k, rwkv_ln_w, rwkv_ln_b, attn_w_qkv, attn_b_qkv, attn_w_o, attn_b_o, attn_sinks, moe_w_router, moe_b_router, moe_w_gu, moe_b_gu, moe_w_down, moe_b_down, final_norm):
    bp, sp, d = x_prompt.shape
    bs, ss, _ = x_sample.shape
    depth = w_mod.shape[0]
    n_ctx = bp * sp
    assert 1 + bs <= MOD_GROUPS and sp % TOKEN_TILE == 0 and ss % 1024 == 0 and n_ctx % ss == 0
    dims = (bp, sp, bs, ss)

    x = jnp.concatenate([x_prompt.reshape(n_ctx, d), x_sample.reshape(bs * ss, d)], axis=0)
    mods_all = _all_mods(c, c_ctx, w_mod, b_mod)

    lru_f, lru_b, rwkv_f, rwkv_b, new_k, new_v = [], [], [], [], [], []
    for layer in range(depth):
        kind, j = layer % 4, layer // 4
        mods = mods_all[layer]
        if kind == 0:
            x, hf, hb = _lru_layer(x, mods, dims, norm_mix[layer], lru_w_in[j], lru_conv_w[j], lru_conv_b[j],
                                   lru_w_a[j], lru_b_a[j], lru_w_x[j], lru_b_x[j], lru_lambda[j], lru_w_out[j],
                                   state_lru_fwd[:, j:j + 1], state_lru_bwd[:, j:j + 1])
            lru_f.append(hf)
            lru_b.append(hb)
        elif kind == 1:
            x = _pool_layer(x, mods, dims, norm_mix[layer], pool_w[j], pool_b[j], pool_scale[j])
        elif kind == 2:
            x, sf, sb = _rwkv_layer(x, mods, dims, norm_mix[layer], rwkv_mix[j], rwkv_w_r[j], rwkv_w_k[j],
                                    rwkv_w_v[j], rwkv_w_o[j], rwkv_dec_w0[j], rwkv_dec_w1[j], rwkv_dec_w2[j],
                                    rwkv_a_w0[j], rwkv_a_w1[j], rwkv_a_w2[j], rwkv_g_w1[j], rwkv_g_w2[j],
                                    rwkv_k_k[j], rwkv_k_a[j], rwkv_r_k[j], rwkv_ln_w[j], rwkv_ln_b[j],
                                    state_rwkv_fwd[:, j], state_rwkv_bwd[:, j])
            rwkv_f.append(sf)
            rwkv_b.append(sb)
        else:
            x, k_c, v_c = _attn_layer(x, mods, dims, norm_mix[layer], attn_w_qkv[j], attn_b_qkv[j], attn_w_o[j],
                                      attn_b_o[j], attn_sinks[j], cache_attn_k[:, j], cache_attn_v[:, j])
            new_k.append(k_c)
            new_v.append(v_c)
        x = _moe_layer(x, norm_ffn[layer], mods, layer, moe_w_router[layer], moe_b_router[layer], moe_w_gu, moe_b_gu,
                       moe_w_down, moe_b_down, n_ctx, ss)

    y = _normmod(x, final_norm, None)
    return (y[:n_ctx].reshape(bp, sp, d), y[n_ctx:].reshape(bs, ss, d),
            jnp.concatenate(lru_f, axis=1), jnp.concatenate(lru_b, axis=1),
            jnp.stack(rwkv_f, axis=1), jnp.stack(rwkv_b, axis=1),
            jnp.stack(new_k, axis=1), jnp.stack(new_v, axis=1))
```

```python
import functools

import jax
import jax.numpy as jnp
from jax import lax
from jax.experimental import pallas as pl
from jax.experimental.pallas import tpu as pltpu

F32 = jnp.float32
BF16 = jnp.bfloat16
I32 = jnp.int32

RMS_EPS = 1e-6
N_MOD = 6
MOD_ROWS = 8
MOD_GROUPS = 8
LRU_BLOCKS = 8
CONV_W = 4
LRU_C = 8.0
POOL_WINDOWS = (2, 4, 8, 16)
RWKV_HEAD = 64
RWKV_GN_EPS = 64e-5
N_TOKEN_MIX = 6
N_HEADS = 32
N_KV_HEADS = 4
KV_GROUP = N_HEADS // N_KV_HEADS
HEAD_DIM = 64
WINDOW = 128
GRID_W = 64
ROPE_BASE = 10000.0
ATTN_SCALE = HEAD_DIM ** -0.5
NEG_INF = -1e30
N_EXPERTS = 32
TOP_K = 4
SWIGLU_LIMIT = 7.0
SWIGLU_ALPHA = 1.702

LANES = 128
TOKEN_TILE = 256
MOE_TILE = 256
COMBINE_TILE = 64
SCAN_CHUNK = 32
VMEM_LIMIT = 56 * 1024 * 1024

HIGHEST = lax.Precision.HIGHEST


def _params(*sem):
    return pltpu.CompilerParams(dimension_semantics=sem, vmem_limit_bytes=VMEM_LIMIT)


def _group_of(tile, tile_rows, n_ctx_rows, lat_seq):
    row = tile * tile_rows
    return jnp.where(row < n_ctx_rows, 0, 1 + (row - n_ctx_rows) // lat_seq)


def _mod_kernel(cond_ref, w_ref, b_ref, o_ref):
    c = cond_ref[...]
    h = c * jax.nn.sigmoid(c)
    o_ref[...] = jnp.dot(h, w_ref[...], preferred_element_type=F32, precision=HIGHEST) + b_ref[...]


def _modulation(cond, w_mod, b_mod):
    depth, d, n = w_mod.shape
    tn = 1024
    return pl.pallas_call(
        _mod_kernel,
        grid=(depth, n // tn),
        in_specs=[
            pl.BlockSpec((MOD_GROUPS, d), lambda l, j: (0, 0)),
            pl.BlockSpec((None, d, tn), lambda l, j: (l, 0, j)),
            pl.BlockSpec((None, 1, tn), lambda l, j: (l, 0, j)),
        ],
        out_specs=pl.BlockSpec((None, MOD_GROUPS, tn), lambda l, j: (l, 0, j)),
        out_shape=jax.ShapeDtypeStruct((depth, MOD_GROUPS, n), F32),
        compiler_params=_params("parallel", "parallel"),
        name="modulation",
    )(cond, w_mod, b_mod.reshape(depth, 1, n))


def _all_mods(c, c_ctx, w_mod, b_mod):
    depth, d, _ = w_mod.shape
    bs = c.shape[0]
    cond = jnp.concatenate([c_ctx[None], c, jnp.zeros((MOD_GROUPS - 1 - bs, d), F32)], axis=0)
    mods = _modulation(cond, w_mod, b_mod).reshape(depth, MOD_GROUPS, N_MOD, d)
    return jnp.pad(mods, ((0, 0), (0, 0), (0, MOD_ROWS - N_MOD), (0, 0)))


def _normmod_kernel(*refs, shift_row, scale_row, modulate, router, slab):
    x_ref, g_ref = refs[0], refs[1]
    pos = 2
    if modulate:
        mod_ref = refs[pos]
        pos += 1
    if router:
        wr_ref, br_ref = refs[pos], refs[pos + 1]
        pos += 2
    o_ref = refs[pos]
    x = x_ref[...]
    ms = jnp.mean(x * x, axis=-1, keepdims=True)
    y = x * lax.rsqrt(ms + RMS_EPS) * g_ref[...]
    if modulate:
        y = y * (1.0 + mod_ref[scale_row:scale_row + 1, :]) + mod_ref[shift_row:shift_row + 1, :]
    if slab:
        n_chunks = y.shape[1] // LANES
        for c in range(n_chunks):
            o_ref[pl.ds(c, y.shape[0], stride=n_chunks), :] = y[:, c * LANES:(c + 1) * LANES]
    else:
        o_ref[...] = y.astype(o_ref.dtype)
    if router:
        l_ref = refs[pos + 1]
        l_ref[...] = jnp.dot(y, wr_ref[...], preferred_element_type=F32, precision=HIGHEST) + br_ref[...]


def _normmod(x, g, mods, *, shift_row=0, scale_row=1, out_dtype=F32, router=None, slab=False, n_ctx_rows=0,
             lat_seq=1):
    t, d = x.shape
    tm = TOKEN_TILE
    modulate = mods is not None
    group = functools.partial(_group_of, tile_rows=tm, n_ctx_rows=n_ctx_rows, lat_seq=lat_seq)
    in_specs = [pl.BlockSpec((tm, d), lambda i: (i, 0)), pl.BlockSpec((1, d), lambda i: (0, 0))]
    args = [x, g.reshape(1, d)]
    if modulate:
        in_specs.append(pl.BlockSpec((None, MOD_ROWS, d), lambda i: (group(i), 0, 0)))
        args.append(mods)
    out_specs = pl.BlockSpec((tm, d), lambda i: (i, 0))
    out_shape = jax.ShapeDtypeStruct((t, d), out_dtype)
    if slab:
        out_specs = pl.BlockSpec((tm * (d // LANES), LANES), lambda i: (i, 0))
        out_shape = jax.ShapeDtypeStruct((t * (d // LANES), LANES), F32)
    if router is not None:
        wr, br = router
        in_specs += [pl.BlockSpec((d, LANES), lambda i: (0, 0)), pl.BlockSpec((1, LANES), lambda i: (0, 0))]
        args += [wr, br]
        out_specs = (out_specs, pl.BlockSpec((tm, LANES), lambda i: (i, 0)))
        out_shape = (out_shape, jax.ShapeDtypeStruct((t, LANES), F32))
    return pl.pallas_call(
        functools.partial(_normmod_kernel, shift_row=shift_row, scale_row=scale_row,
                          modulate=modulate, router=router is not None, slab=slab),
        grid=(t // tm,),
        in_specs=in_specs,
        out_specs=out_specs,
        out_shape=out_shape,
        compiler_params=_params("parallel"),
        name="normmod",
    )(*args)


def _mm_kernel(*refs, out_act, has_bias, has_res, gate_row):
    a_ref, w_ref = refs[0], refs[1]
    pos = 2
    if has_bias:
        b_ref = refs[pos]
        pos += 1
    if has_res:
        x_ref, mod_ref = refs[pos], refs[pos + 1]
        pos += 2
    o_ref, wbf_ref = refs[pos], refs[pos + 1]

    @pl.when(pl.program_id(1) == 0)
    def _():
        wbf_ref[...] = w_ref[...].astype(BF16)

    acc = jnp.dot(a_ref[...].astype(BF16), wbf_ref[...], preferred_element_type=F32)
    if has_bias:
        acc = acc + b_ref[...]
    if out_act == "tanh":
        acc = jnp.tanh(acc)
    elif out_act == "sigmoid":
        acc = jax.nn.sigmoid(acc)
    if has_res:
        acc = x_ref[...] + mod_ref[gate_row:gate_row + 1, :] * acc
    o_ref[...] = acc.astype(o_ref.dtype)


def _matmul(a, w, bias=None, *, out_act=None, out_dtype=F32, res=None, tm=1024, tn=512):
    m, k = a.shape
    n = w.shape[1]
    tn = min(tn, n)
    tm = min(tm, m)
    in_specs = [pl.BlockSpec((tm, k), lambda j, i: (i, 0)), pl.BlockSpec((k, tn), lambda j, i: (0, j))]
    args = [a, w]
    if bias is not None:
        in_specs.append(pl.BlockSpec((1, tn), lambda j, i: (0, j)))
        args.append(bias.reshape(1, n))
    gate_row = 0
    if res is not None:
        x, mods, gate_row, n_ctx_rows, lat_seq = res
        group = functools.partial(_group_of, tile_rows=tm, n_ctx_rows=n_ctx_rows, lat_seq=lat_seq)
        in_specs += [pl.BlockSpec((tm, tn), lambda j, i: (i, j)),
                     pl.BlockSpec((None, MOD_ROWS, tn), lambda j, i: (group(i), 0, j))]
        args += [x, mods]
    return pl.pallas_call(
        functools.partial(_mm_kernel, out_act=out_act, has_bias=bias is not None,
                          has_res=res is not None, gate_row=gate_row),
        grid=(n // tn, m // tm),
        in_specs=in_specs,
        out_specs=pl.BlockSpec((tm, tn), lambda j, i: (i, j)),
        out_shape=jax.ShapeDtypeStruct((m, n), out_dtype),
        scratch_shapes=[pltpu.VMEM((k, tn), BF16)],
        compiler_params=_params("parallel", "arbitrary"),
        name="matmul",
    )(*args)


def _softplus(z):
    return jnp.maximum(z, 0.0) + jnp.log1p(jnp.exp(-jnp.abs(z)))


def _neg_expm1(z):
    t = jnp.tanh(0.5 * z)
    return -2.0 * t / (1.0 - t)


def _lru_kernel(gate_ref, xin_ref, cw_ref, cb_ref, wa_ref, ba_ref, wx_ref, bx_ref, lam_ref,
                h0f_ref, h0b_ref, y_ref, hf_ref, hb_ref, a_sc, u_sc, hs_sc):
    seq = xin_ref.shape[0]
    x = xin_ref[...]
    t_idx = lax.broadcasted_iota(I32, x.shape, 0)

    def shifted(off):
        if off == 0:
            return x
        rolled = pltpu.roll(x, (-off) % seq, 0)
        valid = (t_idx + off >= 0) & (t_idx + off < seq)
        return jnp.where(valid, rolled, 0.0)

    xc = cb_ref[...]
    for j in range(CONV_W):
        xc = xc + cw_ref[j:j + 1, :] * shifted(j - CONV_W // 2)
    xcb = xc.astype(BF16)
    for d in range(2):
        r = jax.nn.sigmoid(jnp.dot(xcb, wa_ref[d].astype(BF16), preferred_element_type=F32)
                           + ba_ref[d:d + 1, :])
        i = jax.nn.sigmoid(jnp.dot(xcb, wx_ref[d].astype(BF16), preferred_element_type=F32)
                           + bx_ref[d:d + 1, :])
        log_a = -LRU_C * r * _softplus(-lam_ref[d:d + 1, :])
        a_sc[d] = jnp.exp(log_a)
        u_sc[d] = jnp.sqrt(_neg_expm1(2.0 * log_a)) * (i * xc)

    def step(t, carry):
        hf, hb = carry
        tb = seq - 1 - t
        hf = a_sc[0, pl.ds(t, 1), :] * hf + u_sc[0, pl.ds(t, 1), :]
        hb = a_sc[1, pl.ds(tb, 1), :] * hb + u_sc[1, pl.ds(tb, 1), :]
        hs_sc[0, pl.ds(t, 1), :] = hf
        hs_sc[1, pl.ds(tb, 1), :] = hb
        return hf, hb

    hf, hb = lax.fori_loop(0, seq, step, (h0f_ref[...], h0b_ref[...]), unroll=8)
    hf_ref[...] = hf
    hb_ref[...] = hb
    y_ref[...] = ((hs_sc[0] + hs_sc[1]) * jax.nn.gelu(gate_ref[...])).astype(y_ref.dtype)


def _lru_scan(gx, n_seq, seq, row_off, conv_w, conv_b, w_a, b_a, w_x, b_x, lam, h0f, h0b):
    d = gx.shape[1] // 2
    c = d // LRU_BLOCKS
    seq_spec = lambda col_off: pl.BlockSpec((seq, c), lambda b, n: (b + row_off, n + col_off))
    vec2 = pl.BlockSpec((2, c), lambda b, n: (0, n))
    state = pl.BlockSpec((None, 1, c), lambda b, n: (b, 0, n))
    gate_w = pl.BlockSpec((2, None, c, c), lambda b, n: (0, n, 0, 0))
    return pl.pallas_call(
        _lru_kernel,
        grid=(n_seq, LRU_BLOCKS),
        in_specs=[seq_spec(0), seq_spec(LRU_BLOCKS),
                  pl.BlockSpec((CONV_W, c), lambda b, n: (0, n)),
                  pl.BlockSpec((1, c), lambda b, n: (0, n)),
                  gate_w, vec2, gate_w, vec2, vec2, state, state],
        out_specs=(pl.BlockSpec((seq, c), lambda b, n: (b, n)), state, state),
        out_shape=(jax.ShapeDtypeStruct((n_seq * seq, d), BF16),
                   jax.ShapeDtypeStruct((n_seq, 1, d), F32),
                   jax.ShapeDtypeStruct((n_seq, 1, d), F32)),
        scratch_shapes=[pltpu.VMEM((2, seq, c), F32)] * 3,
        compiler_params=_params("parallel", "parallel"),
        name="rglru",
    )(gx, gx, conv_w, conv_b.reshape(1, d), w_a, b_a, w_x, b_x, lam, h0f, h0b)


POOL_PAD = max(POOL_WINDOWS) // 2


def _pool_kernel(h_ref, w_ref, b_ref, sc_ref, x_ref, mod_ref, o_ref, xe_ref, *, gate_row):
    seq, c = h_ref.shape
    h = h_ref[...]
    zeros = jnp.zeros((POOL_PAD, c), F32)
    xe_ref[0:POOL_PAD, :] = zeros
    xe_ref[POOL_PAD + seq:POOL_PAD + seq + POOL_PAD, :] = zeros
    xe_ref[POOL_PAD:POOL_PAD + seq, :] = h
    t = lax.broadcasted_iota(I32, (seq, c), 0)
    grp = pl.program_id(1)
    for k, win in enumerate(POOL_WINDOWS):
        @pl.when(grp == k)
        def _(win=win):
            left = win // 2
            right = win - 1 - left
            total = xe_ref[POOL_PAD - left:POOL_PAD - left + seq, :]
            for off in range(-left + 1, right + 1):
                total = total + xe_ref[POOL_PAD + off:POOL_PAD + off + seq, :]
            count = (jnp.minimum(t + right, seq - 1) - jnp.maximum(t - left, 0) + 1).astype(F32)
            dpool = (total / count - h).astype(BF16)
            y = jnp.dot(dpool, w_ref[...].astype(BF16), preferred_element_type=F32) + b_ref[...]
            o_ref[...] = x_ref[...] + mod_ref[gate_row:gate_row + 1, :] * (y * sc_ref[...])


def _pool_mixer(h, x, mods, gate_row, w, b, scale, n_seq, seq, row_off):
    d = h.shape[1]
    ng = len(POOL_WINDOWS)
    c = d // ng
    seq_spec = pl.BlockSpec((seq, c), lambda s, g: (s + row_off, g))
    return pl.pallas_call(
        functools.partial(_pool_kernel, gate_row=gate_row),
        grid=(n_seq, ng),
        in_specs=[seq_spec,
                  pl.BlockSpec((None, c, c), lambda s, g: (g, 0, 0)),
                  pl.BlockSpec((None, 1, c), lambda s, g: (g, 0, 0)),
                  pl.BlockSpec((1, c), lambda s, g: (0, g)),
                  seq_spec,
                  pl.BlockSpec((None, MOD_ROWS, c), lambda s, g: (0 if row_off == 0 else 1 + s, 0, g))],
        out_specs=pl.BlockSpec((seq, c), lambda s, g: (s, g)),
        out_shape=jax.ShapeDtypeStruct((n_seq * seq, d), F32),
        scratch_shapes=[pltpu.VMEM((seq + 2 * POOL_PAD, c), F32)],
        compiler_params=_params("parallel", "parallel"),
        name="pool_mixer",
    )(h, w, b.reshape(ng, 1, c), scale.reshape(1, d), x, mods)


def _shiftmix_kernel(h_ref, mix_ref, o_ref):
    seq = h_ref.shape[0]
    h = h_ref[...]
    t = lax.broadcasted_iota(I32, h.shape, 0)
    prev = jnp.where(t >= 1, pltpu.roll(h, 1, 0), 0.0)
    nxt = jnp.where(t < seq - 1, pltpu.roll(h, seq - 1, 0), 0.0)
    xx = 0.5 * (prev + nxt) - h
    for n in range(N_TOKEN_MIX):
        o_ref[n] = (h + xx * mix_ref[n:n + 1, :]).astype(o_ref.dtype)


def _shiftmix(h, mix, n_seq, seq, row_off, c=512):
    t, d = h.shape
    return pl.pallas_call(
        _shiftmix_kernel,
        grid=(n_seq, d // c),
        in_specs=[pl.BlockSpec((seq, c), lambda s, j: (s + row_off, j)),
                  pl.BlockSpec((N_TOKEN_MIX, c), lambda s, j: (0, j))],
        out_specs=pl.BlockSpec((N_TOKEN_MIX, seq, c), lambda s, j: (0, s, j)),
        out_shape=jax.ShapeDtypeStruct((N_TOKEN_MIX, n_seq * seq, d), BF16),
        compiler_params=_params("parallel", "parallel"),
        name="rwkv_shiftmix",
    )(h, mix)


def _head_sums(x):
    r = lax.broadcasted_iota(I32, (LANES, LANES), 0) // RWKV_HEAD
    c = lax.broadcasted_iota(I32, (LANES, LANES), 1) // RWKV_HEAD
    ones = jnp.where(r == c, 1.0, 0.0).astype(BF16)
    outs = []
    for j in range(x.shape[1] // LANES):
        xs = x[:, j * LANES:(j + 1) * LANES]
        hi = xs.astype(BF16)
        rem = xs - hi.astype(F32)
        mid = rem.astype(BF16)
        lo = (rem - mid.astype(F32)).astype(BF16)
        outs.append(jnp.dot(hi, ones, preferred_element_type=F32)
                    + jnp.dot(mid, ones, preferred_element_type=F32)
                    + jnp.dot(lo, ones, preferred_element_type=F32))
    return jnp.concatenate(outs, axis=-1)


def _rwkv_prep_kernel(r_ref, k_ref, v_ref, dec_ref, aa_ref, kk_ref, ka_ref, rk_ref,
                      a_ref, w_ref, kd_ref, b_ref, bonus_ref):
    d = r_ref.shape[1]
    r = r_ref[...]
    k = k_ref[...]
    kk = k * kk_ref[...]
    kk = kk / jnp.maximum(jnp.sqrt(_head_sums(kk * kk)), 1e-12)
    a_ref[...] = -kk
    for dr in range(2):
        w_log = -_softplus(-dec_ref[:, dr * d:(dr + 1) * d]) - 0.5
        w_ref[dr] = jnp.exp(-jnp.exp(w_log))
        iclr = jax.nn.sigmoid(aa_ref[:, dr * d:(dr + 1) * d])
        kd_ref[dr] = k * (1.0 + (iclr - 1.0) * ka_ref[...])
        b_ref[dr] = kk * iclr
    bonus_ref[...] = _head_sums(r * k * rk_ref[...]) * v_ref[...]


def _rwkv_prep(r, k, v, dec, aa, k_k, k_a, r_k):
    t, d = r.shape
    tm = TOKEN_TILE // 2
    row = pl.BlockSpec((tm, d), lambda i: (i, 0))
    row2 = pl.BlockSpec((tm, 2 * d), lambda i: (i, 0))
    vec = pl.BlockSpec((1, d), lambda i: (0, 0))
    both = pl.BlockSpec((2, tm, d), lambda i: (0, i, 0))
    one = jax.ShapeDtypeStruct((t, d), F32)
    two = jax.ShapeDtypeStruct((2, t, d), F32)
    return pl.pallas_call(
        _rwkv_prep_kernel,
        grid=(t // tm,),
        in_specs=[row, row, row, row2, row2, vec, vec, vec],
        out_specs=(row, both, both, both, row),
        out_shape=(one, two, two, two, one),
        compiler_params=_params("parallel"),
        name="rwkv_prep",
    )(r, k, v, dec, aa, k_k.reshape(1, d), k_a.reshape(1, d), r_k.reshape(1, d))


SCAN_VB = 8


def _rwkv_scan_kernel(r_ref, v_ref, a_ref, w_ref, k_ref, b_ref, s0_ref, y_ref, st_ref, state, *, groups_per_dir):
    chunk = pl.program_id(1)
    n_t, n_k = r_ref.shape[0], r_ref.shape[1]
    n_vb = state.shape[0]
    backward = pl.program_id(0) >= groups_per_dir

    @pl.when(chunk == 0)
    def _():
        state[...] = s0_ref[...]

    def tree_sum(parts):
        while len(parts) > 1:
            parts = [parts[i] + parts[i + 1] for i in range(0, len(parts), 2)]
        return parts[0]

    def t_step(step, _):
        t = jnp.where(backward, n_t - 1 - step, step)

        def vb_step(vb, _):
            parts = [None] * 4
            for kk in range(n_k):
                term = state[vb, kk] * a_ref[t, kk:kk + 1, :]
                parts[kk % 4] = term if parts[kk % 4] is None else parts[kk % 4] + term
            sa = tree_sum(parts)
            rows = pl.ds(pl.multiple_of(vb * SCAN_VB, SCAN_VB), SCAN_VB)
            vv = v_ref[t, rows, :]
            parts = [None] * 4
            for kk in range(n_k):
                s = (state[vb, kk] * w_ref[t, kk:kk + 1, :] + sa * b_ref[t, kk:kk + 1, :]
                     + vv * k_ref[t, kk:kk + 1, :])
                state[vb, kk] = s
                term = s * r_ref[t, kk:kk + 1, :]
                parts[kk % 4] = term if parts[kk % 4] is None else parts[kk % 4] + term
            y_ref[t, rows, :] = tree_sum(parts)
            return 0

        lax.fori_loop(0, n_vb, vb_step, 0)
        return 0

    lax.fori_loop(0, n_t, t_step, 0)

    @pl.when(chunk == pl.num_programs(1) - 1)
    def _():
        st_ref[...] = state[...]


def _rwkv_scan(r, v, a, w, k, b, s0, *, two_way):
    seq, n, chains = r.shape
    n_dir = w.shape[0]
    assert n_dir == (2 if two_way else 1)
    tc = min(SCAN_CHUNK, seq)
    n_chunks = seq // tc
    gpd = chains // LANES
    groups_per_dir = gpd if two_way else n_dir * gpd

    def chunk_of(g, c):
        return jnp.where(g >= groups_per_dir, n_chunks - 1 - c, c)

    shared = pl.BlockSpec((tc, n, LANES), lambda g, c: (chunk_of(g, c), 0, g % gpd))
    per_dir = pl.BlockSpec((None, tc, n, LANES), lambda g, c: (g // gpd, chunk_of(g, c), 0, g % gpd))
    st_spec = pl.BlockSpec((n // SCAN_VB, n, SCAN_VB, LANES), lambda g, c: (0, 0, 0, g))
    return pl.pallas_call(
        functools.partial(_rwkv_scan_kernel, groups_per_dir=groups_per_dir),
        grid=(n_dir * gpd, n_chunks),
        in_specs=[shared] * 3 + [per_dir] * 3 + [st_spec],
        out_specs=(per_dir, st_spec),
        out_shape=(jax.ShapeDtypeStruct((n_dir, seq, n, chains), F32),
                   jax.ShapeDtypeStruct((n // SCAN_VB, n, SCAN_VB, n_dir * chains), F32)),
        scratch_shapes=[pltpu.VMEM((n // SCAN_VB, n, SCAN_VB, LANES), F32)],
        compiler_params=_params("parallel", "arbitrary"),
        name="rwkv_scan",
    )(r, v, a, w, k, b, s0)


def _rwkv_post_kernel(yf_ref, yb_ref, bonus_ref, g_ref, lnw_ref, lnb_ref, o_ref):
    y = yf_ref[...] + yb_ref[...]
    mu = _head_sums(y) * (1.0 / RWKV_HEAD)
    yc = y - mu
    var = _head_sums(yc * yc) * (1.0 / RWKV_HEAD)
    y = yc * lax.rsqrt(var + RWKV_GN_EPS) * lnw_ref[...] + lnb_ref[...]
    o_ref[...] = ((y + bonus_ref[...]) * g_ref[...]).astype(o_ref.dtype)


def _rwkv_post(yf, yb, bonus, g, ln_w, ln_b):
    t, d = yf.shape
    tm = TOKEN_TILE
    row = pl.BlockSpec((tm, d), lambda i: (i, 0))
    vec = pl.BlockSpec((1, d), lambda i: (0, 0))
    return pl.pallas_call(
        _rwkv_post_kernel,
        grid=(t // tm,),
        in_specs=[row, row, row, row, vec, vec],
        out_specs=row,
        out_shape=jax.ShapeDtypeStruct((t, d), BF16),
        compiler_params=_params("parallel"),
        name="rwkv_post",
    )(yf, yb, bonus, g, ln_w.reshape(1, d), ln_b.reshape(1, d))


def _rope_kernel(x_ref, cos_ref, sin_ref, o_ref):
    x = x_ref[...]
    hd = x.shape[1]
    quarter = hd // 4
    src = lax.broadcasted_iota(I32, (hd, hd), 0)
    dst = lax.broadcasted_iota(I32, (hd, hd), 1)
    partner = jnp.where((dst // quarter) % 2 == 0, dst + quarter, dst - quarter)
    perm = jnp.where(src == partner, 1.0, 0.0).astype(BF16)
    hi = x.astype(BF16)
    rem = x - hi.astype(F32)
    mid = rem.astype(BF16)
    lo = (rem - mid.astype(F32)).astype(BF16)
    swapped = (jnp.dot(hi, perm, preferred_element_type=F32)
               + jnp.dot(mid, perm, preferred_element_type=F32)
               + jnp.dot(lo, perm, preferred_element_type=F32))
    o_ref[...] = (x * cos_ref[...] + swapped * sin_ref[...]).astype(o_ref.dtype)


def _rope(x, cos, sin):
    n, seq, hd = x.shape
    tab = pl.BlockSpec((seq, hd), lambda i: (0, 0))
    return pl.pallas_call(
        _rope_kernel,
        grid=(n,),
        in_specs=[pl.BlockSpec((None, seq, hd), lambda i: (i, 0, 0)), tab, tab],
        out_specs=pl.BlockSpec((None, seq, hd), lambda i: (i, 0, 0)),
        out_shape=jax.ShapeDtypeStruct((n, seq, hd), BF16),
        compiler_params=_params("parallel"),
        name="rope",
    )(x, cos, sin)


def _attn_kernel(sink_ref, q_ref, k_ref, v_ref, o_ref, *, n_cache, window):
    n_g, tq = q_ref.shape[0], q_ref.shape[1]
    n_k = k_ref.shape[0]
    k = k_ref[...].astype(BF16)
    v = v_ref[...].astype(BF16)
    if window is not None:
        q_pos = pl.program_id(2) * tq + lax.broadcasted_iota(I32, (tq, n_k), 0)
        col = lax.broadcasted_iota(I32, (tq, n_k), 1)
        valid = (col < n_cache) | (jnp.abs(q_pos - (col - n_cache)) <= window)
    for g in range(n_g):
        sink = sink_ref[pl.program_id(1) * n_g + g]
        s = lax.dot_general(q_ref[g].astype(BF16), k, (((1,), (1,)), ((), ())),
                            preferred_element_type=F32) * ATTN_SCALE
        if window is not None:
            s = jnp.where(valid, s, NEG_INF)
        m = jnp.maximum(jnp.max(s, axis=-1, keepdims=True), sink)
        p = jnp.exp(s - m)
        denom = jnp.sum(p, axis=-1, keepdims=True) + jnp.exp(sink - m)
        o = jnp.dot(p.astype(BF16), v, preferred_element_type=F32)
        o_ref[g] = (o / denom).astype(o_ref.dtype)


def _attention(q, k, v, sinks, *, n_cache=0, window=None, tq=256):
    b, kv, g, sq, hd = q.shape
    sk = k.shape[2]
    kv_spec = pl.BlockSpec((None, None, sk, hd), lambda bi, ki, qi, s: (bi, ki, 0, 0))
    q_spec = pl.BlockSpec((None, None, g, tq, hd), lambda bi, ki, qi, s: (bi, ki, 0, qi, 0))
    return pl.pallas_call(
        functools.partial(_attn_kernel, n_cache=n_cache, window=window),
        grid_spec=pltpu.PrefetchScalarGridSpec(
            num_scalar_prefetch=1,
            grid=(b, kv, sq // tq),
            in_specs=[q_spec, kv_spec, kv_spec],
            out_specs=q_spec),
        out_shape=jax.ShapeDtypeStruct(q.shape, BF16),
        compiler_params=_params("parallel", "parallel", "parallel"),
        name="attention",
    )(sinks, q, k, v)


def _rope_tables(seq):
    quarter = HEAD_DIM // 4
    pos = jnp.arange(seq)
    rows = (pos // GRID_W).astype(F32)
    cols = (pos % GRID_W).astype(F32)
    inv_freq = ROPE_BASE ** (-jnp.arange(quarter, dtype=F32) / quarter)
    ang_r = rows[:, None] * inv_freq
    ang_c = cols[:, None] * inv_freq
    cos = jnp.concatenate([jnp.cos(ang_r)] * 2 + [jnp.cos(ang_c)] * 2, axis=1)
    sin = jnp.concatenate([-jnp.sin(ang_r), jnp.sin(ang_r), -jnp.sin(ang_c), jnp.sin(ang_c)], axis=1)
    return cos, sin


def _topk_kernel(l_ref, e_ref, g_ref):
    x = l_ref[...]
    lane = lax.broadcasted_iota(I32, x.shape, 1)
    lane_f = lane.astype(F32)
    x = jnp.where(lane < N_EXPERTS, x, -jnp.inf)
    e_out = jnp.zeros(x.shape, F32)
    tops = []
    for r in range(TOP_K):
        m = jnp.max(x, axis=-1, keepdims=True)
        idx = jnp.min(jnp.where(x == m, lane_f, float(LANES)), axis=-1, keepdims=True)
        e_out = jnp.where(lane == r, idx, e_out)
        x = jnp.where(lane_f == idx, -jnp.inf, x)
        tops.append(m)
    ex = [jnp.exp(m - tops[0]) for m in tops]
    tot = ex[0] + ex[1] + ex[2] + ex[3]
    g_out = jnp.zeros(x.shape, F32)
    for r in range(TOP_K):
        g_out = jnp.where(lane == r, ex[r] / tot, g_out)
    e_ref[...] = e_out.astype(I32)
    g_ref[...] = g_out


def _topk(logits):
    t = logits.shape[0]
    tm = TOKEN_TILE
    spec = pl.BlockSpec((tm, LANES), lambda i: (i, 0))
    return pl.pallas_call(
        _topk_kernel,
        grid=(t // tm,),
        in_specs=[spec],
        out_specs=(spec, spec),
        out_shape=(jax.ShapeDtypeStruct((t, LANES), I32), jax.ShapeDtypeStruct((t, LANES), F32)),
        compiler_params=_params("parallel"),
        name="router_topk",
    )(logits)


def _rank_kernel(e_ref, rank_ref, cnt_ref, carry):
    @pl.when(pl.program_id(0) == 0)
    def _():
        carry[...] = jnp.zeros(carry.shape, F32)

    e = e_ref[...]
    tm = e.shape[0]
    lane = lax.broadcasted_iota(I32, e.shape, 1)
    hits = []
    member = jnp.zeros(e.shape, F32)
    for r in range(TOP_K):
        hit = lane == e[:, r:r + 1]
        hits.append(hit)
        member = member + jnp.where(hit, 1.0, 0.0)
    row = lax.broadcasted_iota(I32, (tm, tm), 0)
    col = lax.broadcasted_iota(I32, (tm, tm), 1)
    tri = jnp.where(row > col, 1.0, 0.0).astype(BF16)
    before = jnp.dot(tri, member.astype(BF16), preferred_element_type=F32) + carry[0:1, :]
    out = jnp.zeros(e.shape, F32)
    for r in range(TOP_K):
        rk = jnp.sum(jnp.where(hits[r], before, 0.0), axis=-1, keepdims=True)
        out = jnp.where(lane == r, rk, out)
    rank_ref[...] = out.astype(I32)
    carry[...] = carry[...] + jnp.sum(member, axis=0, keepdims=True)
    cnt_ref[...] = carry[...]


def _rank(top_e):
    t = top_e.shape[0]
    tm = TOKEN_TILE
    spec = pl.BlockSpec((tm, LANES), lambda i: (i, 0))
    cnt_spec = pl.BlockSpec((8, LANES), lambda i: (0, 0))
    return pl.pallas_call(
        _rank_kernel,
        grid=(t // tm,),
        in_specs=[spec],
        out_specs=(spec, cnt_spec),
        out_shape=(jax.ShapeDtypeStruct((t, LANES), I32), jax.ShapeDtypeStruct((8, LANES), F32)),
        scratch_shapes=[pltpu.VMEM((8, LANES), F32)],
        compiler_params=_params("arbitrary"),
        name="router_rank",
    )(top_e)


def _slab_copy(src_hbm, row, buf, slot, sem):
    n_chunks = src_hbm.shape[1]
    return pltpu.make_async_copy(src_hbm.at[row], buf.at[pl.ds(slot * n_chunks, n_chunks), :], sem)


def _gather_kernel(idx_ref, src_hbm, o_ref, buf, sem):
    n = o_ref.shape[0]
    n_chunks = src_hbm.shape[1]
    i = pl.program_id(0)
    slot = i % 2

    def issue_block(blk, sl):
        def issue(r, _):
            _slab_copy(src_hbm, idx_ref[blk * n + r], buf.at[sl], r, sem.at[sl]).start()
            return 0

        lax.fori_loop(0, n, issue, 0, unroll=8)

    @pl.when(i == 0)
    def _():
        issue_block(0, 0)

    @pl.when(i + 1 < pl.num_programs(0))
    def _():
        issue_block(i + 1, 1 - slot)

    def drain(r, _):
        _slab_copy(src_hbm, 0, buf.at[slot], r, sem.at[slot]).wait()
        return 0

    lax.fori_loop(0, n, drain, 0, unroll=8)
    for c in range(n_chunks):
        o_ref[:, c * LANES:(c + 1) * LANES] = buf[slot, pl.ds(c, n, stride=n_chunks), :].astype(o_ref.dtype)


def _gather_rows(src, idx, out_dtype):
    n = idx.shape[0]
    n_chunks = src.shape[1]
    tm = MOE_TILE
    return pl.pallas_call(
        _gather_kernel,
        grid_spec=pltpu.PrefetchScalarGridSpec(
            num_scalar_prefetch=1,
            grid=(n // tm,),
            in_specs=[pl.BlockSpec(memory_space=pl.ANY)],
            out_specs=pl.BlockSpec((tm, n_chunks * LANES), lambda i, idx: (i, 0)),
            scratch_shapes=[pltpu.VMEM((2, tm * n_chunks, LANES), src.dtype), pltpu.SemaphoreType.DMA((2,))]),
        out_shape=jax.ShapeDtypeStruct((n, n_chunks * LANES), out_dtype),
        compiler_params=_params("arbitrary"),
        name="moe_gather",
    )(idx, src)


def _gmm_kernel(exp_ref, first_ref, slot_ref, nxt_e_ref, nxt_j_ref, valid_ref, blk_ref, *refs,
                swiglu, layer, tn, n_blocks, slab_out):
    if swiglu:
        x_ref, w_hbm, bg_ref, bu_ref, o_ref, wbuf, wbf, sem = refs
    else:
        x_ref, w_hbm, bg_ref, o_ref, wbuf, wbf, sem = refs
    n_w = 2 if swiglu else 1
    n_half = w_hbm.shape[3] // n_w
    s = pl.program_id(0)

    def w_copy(e, j, slot, which):
        col = pl.multiple_of(j * tn + which * n_half, LANES)
        return pltpu.make_async_copy(w_hbm.at[layer, e, :, pl.ds(col, tn)], wbuf.at[slot, which],
                                     sem.at[slot, which])

    @pl.when(valid_ref[s] == 1)
    def _():
        @pl.when(first_ref[s] == 1)
        def _():
            slot = slot_ref[s]

            @pl.when(s == 0)
            def _():
                for which in range(n_w):
                    w_copy(exp_ref[s], 0, slot, which).start()

            for which in range(n_w):
                w_copy(exp_ref[s], s // n_blocks, slot, which).wait()
                wbf[which] = wbuf[slot, which].astype(BF16)

            @pl.when(nxt_e_ref[s] >= 0)
            def _():
                for which in range(n_w):
                    w_copy(nxt_e_ref[s], nxt_j_ref[s], 1 - slot, which).start()

        x = x_ref[...]
        g = jnp.dot(x, wbf[0], preferred_element_type=F32) + bg_ref[...]
        if swiglu:
            u = jnp.dot(x, wbf[1], preferred_element_type=F32) + bu_ref[...]
            g = jnp.minimum(g, SWIGLU_LIMIT)
            u = jnp.clip(u, -SWIGLU_LIMIT, SWIGLU_LIMIT)
            g = (u + 1.0) * (g * jax.nn.sigmoid(SWIGLU_ALPHA * g))
        if slab_out:
            n_chunks = tn // LANES
            for c in range(n_chunks):
                o_ref[pl.ds(c, g.shape[0], stride=n_chunks), :] = g[:, c * LANES:(c + 1) * LANES]
        else:
            o_ref[...] = g.astype(o_ref.dtype)

    @pl.when(valid_ref[s] == 0)
    def _():
        o_ref[...] = jnp.zeros(o_ref.shape, o_ref.dtype)


def _grouped_matmul(x, w, b, layer, sched, *, swiglu, out_dtype, tn, slab_out=False):
    p, k = x.shape
    n_l, n_e = w.shape[:2]
    n_w = 2 if swiglu else 1
    n = w.shape[3] // n_w
    tm = MOE_TILE
    nj = n // tn
    n_blocks = p // tm
    assert not slab_out or nj == 1
    b4 = b.reshape(n_l, n_e, 1, b.shape[2])
    x_spec = pl.BlockSpec((tm, k), lambda s, ex, fi, sl, ne, njx, va, bl: (bl[s], 0))
    b_spec = lambda off: pl.BlockSpec((None, None, 1, tn),
                                      lambda s, ex, fi, sl, ne, njx, va, bl: (layer, ex[s], 0, s // n_blocks + off))
    in_specs = [x_spec, pl.BlockSpec(memory_space=pl.ANY), b_spec(0)]
    args = [x, w, b4]
    if swiglu:
        in_specs.append(b_spec(nj))
        args.append(b4)
    if slab_out:
        out_spec = pl.BlockSpec((tm * (n // LANES), LANES), lambda s, ex, fi, sl, ne, njx, va, bl: (s, 0))
        out_shape = jax.ShapeDtypeStruct((p * (n // LANES), LANES), out_dtype)
    else:
        out_spec = pl.BlockSpec((tm, tn), lambda s, ex, fi, sl, ne, njx, va, bl: (s % n_blocks, s // n_blocks))
        out_shape = jax.ShapeDtypeStruct((p, n), out_dtype)
    return pl.pallas_call(
        functools.partial(_gmm_kernel, swiglu=swiglu, layer=layer, tn=tn, n_blocks=n_blocks, slab_out=slab_out),
        grid_spec=pltpu.PrefetchScalarGridSpec(
            num_scalar_prefetch=7,
            grid=(nj * n_blocks,),
            in_specs=in_specs,
            out_specs=out_spec,
            scratch_shapes=[pltpu.VMEM((2, n_w, k, tn), F32), pltpu.VMEM((n_w, k, tn), BF16),
                            pltpu.SemaphoreType.DMA((2, n_w))]),
        out_shape=out_shape,
        compiler_params=_params("arbitrary"),
        name="moe_experts",
    )(*sched(nj), *args)


def _moe_schedule(blocks_per_e, n_blocks):
    blk_end = jnp.cumsum(blocks_per_e)
    n_used = blk_end[-1]
    i = jnp.arange(n_blocks, dtype=I32)
    blk_i = jnp.minimum(i, n_used - 1)
    exp_i = jnp.sum((blk_end[None, :] <= blk_i[:, None]).astype(I32), axis=1)
    valid_i = i < n_used
    first_i = valid_i & jnp.concatenate([jnp.ones((1,), bool), exp_i[1:] != exp_i[:-1]])

    def sched(nj):
        n_steps = nj * n_blocks
        exp = jnp.tile(exp_i, nj)
        first = jnp.tile(first_i, nj)
        step = jnp.arange(n_steps, dtype=I32)
        slot = (jnp.cumsum(first.astype(I32)) - 1) % 2
        pos = jnp.where(first, step, n_steps)
        nxt = jnp.concatenate([lax.cummin(pos[::-1])[::-1][1:], jnp.full((1,), n_steps, I32)])
        has_nxt = nxt < n_steps
        nxt_c = jnp.minimum(nxt, n_steps - 1)
        nxt_e = jnp.where(has_nxt, exp[nxt_c], -1)
        nxt_j = nxt_c // n_blocks
        return (exp, first.astype(I32), slot.astype(I32), nxt_e.astype(I32), nxt_j.astype(I32),
                jnp.tile(valid_i, nj).astype(I32), jnp.tile(blk_i, nj))

    return sched


def _combine_kernel(dest_ref, ys_hbm, gate_ref, x_ref, mod_ref, o_ref, buf, sem, *, gate_row):
    tm = x_ref.shape[0]
    n = tm * TOP_K
    n_chunks = ys_hbm.shape[1]
    i = pl.program_id(0)
    slot = i % 2

    def issue_block(blk, sl):
        def issue(r, _):
            _slab_copy(ys_hbm, dest_ref[blk * n + r], buf.at[sl], r, sem.at[sl]).start()
            return 0

        lax.fori_loop(0, n, issue, 0, unroll=8)

    @pl.when(i == 0)
    def _():
        issue_block(0, 0)

    @pl.when(i + 1 < pl.num_programs(0))
    def _():
        issue_block(i + 1, 1 - slot)

    def drain(r, _):
        _slab_copy(ys_hbm, 0, buf.at[slot], r, sem.at[slot]).wait()
        return 0

    lax.fori_loop(0, n, drain, 0, unroll=8)
    gates = [jnp.broadcast_to(gate_ref[:, k:k + 1], (tm, LANES)) for k in range(TOP_K)]
    for c in range(n_chunks):
        acc = gates[0] * buf[slot, pl.ds(c, tm, stride=n_chunks), :]
        for k in range(1, TOP_K):
            acc = acc + gates[k] * buf[slot, pl.ds(k * tm * n_chunks + c, tm, stride=n_chunks), :]
        cols = slice(c * LANES, (c + 1) * LANES)
        o_ref[:, cols] = x_ref[:, cols] + mod_ref[gate_row:gate_row + 1, cols] * acc


def _combine(ys, dest_blocked, gates, x, mods, gate_row, n_ctx_rows, lat_seq):
    t, d = x.shape
    tm = COMBINE_TILE
    group = functools.partial(_group_of, tile_rows=tm, n_ctx_rows=n_ctx_rows, lat_seq=lat_seq)
    return pl.pallas_call(
        functools.partial(_combine_kernel, gate_row=gate_row),
        grid_spec=pltpu.PrefetchScalarGridSpec(
            num_scalar_prefetch=1,
            grid=(t // tm,),
            in_specs=[pl.BlockSpec(memory_space=pl.ANY),
                      pl.BlockSpec((tm, LANES), lambda i, de: (i, 0)),
                      pl.BlockSpec((tm, d), lambda i, de: (i, 0)),
                      pl.BlockSpec((None, MOD_ROWS, d), lambda i, de: (group(i), 0, 0))],
            out_specs=pl.BlockSpec((tm, d), lambda i, de: (i, 0)),
            scratch_shapes=[pltpu.VMEM((2, tm * TOP_K * (d // LANES), LANES), F32),
                            pltpu.SemaphoreType.DMA((2,))]),
        out_shape=jax.ShapeDtypeStruct((t, d), F32),
        compiler_params=_params("arbitrary"),
        name="moe_combine",
    )(dest_blocked, ys, gates, x, mods)


def _moe_layer(x, g_norm, mods, layer, w_router, b_router, w_gu, b_gu, w_down, b_down, n_ctx_rows, lat_seq):
    t, d = x.shape
    n_e = w_router.shape[1]
    n_chunks = d // LANES
    wr = jnp.pad(w_router, ((0, 0), (0, LANES - n_e)))
    br = jnp.pad(b_router, (0, LANES - n_e)).reshape(1, LANES)
    h, logits = _normmod(x, g_norm, mods, shift_row=3, scale_row=4, router=(wr, br), slab=True,
                         n_ctx_rows=n_ctx_rows, lat_seq=lat_seq)
    top_e, gates = _topk(logits)
    rank, counts = _rank(top_e)

    counts = counts[0, :n_e].astype(I32)
    blocks_per_e = (counts + MOE_TILE - 1) // MOE_TILE
    seg_start = (jnp.cumsum(blocks_per_e) - blocks_per_e) * MOE_TILE
    n_blocks = -(-(t * TOP_K) // MOE_TILE) + n_e
    sched = _moe_schedule(blocks_per_e, n_blocks)
    dest = seg_start[top_e[:, :TOP_K]] + rank[:, :TOP_K]
    tok = jnp.broadcast_to(jnp.arange(t, dtype=I32)[:, None], (t, TOP_K))
    src_tok = jnp.zeros((n_blocks * MOE_TILE,), I32).at[dest.reshape(-1)].set(tok.reshape(-1))
    dest_blocked = dest.reshape(t // COMBINE_TILE, COMBINE_TILE, TOP_K).transpose(0, 2, 1).reshape(-1)

    xs = _gather_rows(h.reshape(t, n_chunks, LANES), src_tok, BF16)
    act = _grouped_matmul(xs, w_gu, b_gu, layer, sched, swiglu=True, out_dtype=BF16, tn=1024)
    ys = _grouped_matmul(act, w_down, b_down, layer, sched, swiglu=False, out_dtype=F32, tn=w_down.shape[3],
                         slab_out=True)
    return _combine(ys.reshape(n_blocks * MOE_TILE, n_chunks, LANES), dest_blocked, gates, x, mods, 5,
                    n_ctx_rows, lat_seq)


def _dims(dims):
    bp, sp, bs, ss = dims
    n_ctx = bp * sp
    return n_ctx, n_ctx // ss, dict(n_ctx_rows=n_ctx, lat_seq=ss)


def _lru_layer(x, mods, dims, g_norm, w_in, conv_w, conv_b, w_a, b_a, w_x, b_x, lam, w_out, h0f, h0b):
    bp, sp, bs, ss = dims
    n_ctx, lat_off, grp = _dims(dims)
    d = x.shape[1]
    h = _normmod(x, g_norm, mods, out_dtype=BF16, **grp)
    gx = _matmul(h, w_in)
    lru = (conv_w, conv_b, w_a, b_a, w_x, b_x, lam)
    zero = jnp.zeros((bp, 1, d), F32)
    y_c, hf, hb = _lru_scan(gx, bp, sp, 0, *lru, zero, zero)
    y_l, _, _ = _lru_scan(gx, bs, ss, lat_off, *lru, h0f, h0b)
    y = jnp.concatenate([y_c, y_l], axis=0)
    return _matmul(y, w_out, res=(x, mods, 2, n_ctx, ss)), hf, hb


def _pool_layer(x, mods, dims, g_norm, w, b, scale):
    bp, sp, bs, ss = dims
    n_ctx, lat_off, grp = _dims(dims)
    h = _normmod(x, g_norm, mods, **grp)
    x_c = _pool_mixer(h, x, mods, 2, w, b, scale, bp, sp, 0)
    x_l = _pool_mixer(h, x, mods, 2, w, b, scale, bs, ss, lat_off)
    return jnp.concatenate([x_c, x_l], axis=0)


def _rwkv_layer(x, mods, dims, g_norm, mix, w_r, w_k, w_v, w_o, dec_w0, dec_w1, dec_w2, a_w0, a_w1, a_w2,
                g_w1, g_w2, k_k, k_a, r_k, ln_w, ln_b, s0f, s0b):
    bp, sp, bs, ss = dims
    n_ctx, lat_off, grp = _dims(dims)
    d = x.shape[1]
    heads = d // RWKV_HEAD
    h = _normmod(x, g_norm, mods, **grp)
    x6 = jnp.concatenate([_shiftmix(h, mix, bp, sp, 0), _shiftmix(h, mix, bs, ss, lat_off)], axis=1)
    xr, xw, xk, xv, xa, xg = (x6[n] for n in range(N_TOKEN_MIX))
    r = _matmul(xr, w_r)
    k = _matmul(xk, w_k)
    v = _matmul(xv, w_v)
    pad = 2 * LANES - 2 * dec_w1.shape[-1]

    def lora_in(w1):
        return jnp.pad(jnp.concatenate([w1[0], w1[1]], axis=1), ((0, 0), (0, pad)))

    def lora_out(w2):
        z = jnp.zeros_like(w2[0])
        top = jnp.concatenate([w2[0], z], axis=1)
        bot = jnp.concatenate([z, w2[1]], axis=1)
        return jnp.pad(jnp.concatenate([top, bot], axis=0), ((0, pad), (0, 0)))

    dec_h = _matmul(xw, lora_in(dec_w1), out_act="tanh", out_dtype=BF16)
    dec = _matmul(dec_h, lora_out(dec_w2), dec_w0.reshape(-1))
    aa_h = _matmul(xa, lora_in(a_w1), out_dtype=BF16)
    aa = _matmul(aa_h, lora_out(a_w2), a_w0.reshape(-1))
    g_h = _matmul(xg, g_w1, out_act="sigmoid", out_dtype=BF16)
    g = _matmul(g_h, g_w2)
    a_neg, w_dec, k_d, b_d, bonus = _rwkv_prep(r, k, v, dec, aa, k_k, k_a, r_k.reshape(-1))

    def scan_group(lo, n_seq, seq, s0_fwd, s0_bwd, two_way):
        nc = n_seq * heads

        def lay(z):
            lead = z.shape[:-2]
            z = z[..., lo:lo + n_seq * seq, :].reshape(lead + (n_seq, seq, heads, RWKV_HEAD))
            nl = len(lead)
            perm = tuple(range(nl)) + (nl + 1, nl + 3, nl, nl + 2)
            return z.transpose(perm).reshape(lead + (seq, RWKV_HEAD, nc))

        def st(z):
            z = z.reshape(nc, RWKV_HEAD // SCAN_VB, SCAN_VB, RWKV_HEAD)
            return z.transpose(1, 3, 2, 0)

        def unlay(z):
            return z.reshape(seq, RWKV_HEAD, n_seq, heads).transpose(2, 0, 3, 1).reshape(n_seq * seq, d)

        def unst(z):
            return z.transpose(3, 0, 2, 1).reshape(n_seq, heads, RWKV_HEAD, RWKV_HEAD)

        s0 = jnp.concatenate([st(s0_fwd), st(s0_bwd)], axis=-1)
        if two_way:
            y, s_fin = _rwkv_scan(lay(r), lay(v), lay(a_neg), lay(w_dec), lay(k_d), lay(b_d), s0, two_way=True)
            y_f, y_b = y[0], y[1]
        else:
            both = lambda zf, zb: jnp.concatenate([lay(zf), lay(zb)[::-1]], axis=-1)
            y, s_fin = _rwkv_scan(both(r, r), both(v, v), both(a_neg, a_neg), both(w_dec[0], w_dec[1])[None],
                                  both(k_d[0], k_d[1])[None], both(b_d[0], b_d[1])[None], s0, two_way=False)
            y_f, y_b = y[0, :, :, :nc], y[0, ::-1, :, nc:]
        return unlay(y_f), unlay(y_b), unst(s_fin[..., :nc]), unst(s_fin[..., nc:])

    zero = jnp.zeros((bp, heads, RWKV_HEAD, RWKV_HEAD), F32)
    yf_c, yb_c, sf, sb = scan_group(0, bp, sp, zero, zero, (bp * heads) % LANES == 0)
    yf_l, yb_l, _, _ = scan_group(n_ctx, bs, ss, s0f, s0b, (bs * heads) % LANES == 0)
    yf = jnp.concatenate([yf_c, yf_l], axis=0)
    yb = jnp.concatenate([yb_c, yb_l], axis=0)
    yg = _rwkv_post(yf, yb, bonus, g, ln_w, ln_b)
    return _matmul(yg, w_o, res=(x, mods, 2, n_ctx, ss)), sf, sb


def _attn_layer(x, mods, dims, g_norm, w_qkv, b_qkv, w_o, b_o, sinks, cache_k, cache_v):
    bp, sp, bs, ss = dims
    n_ctx, lat_off, grp = _dims(dims)
    h = _normmod(x, g_norm, mods, out_dtype=BF16, **grp)
    qkv = _matmul(h, w_qkv, b_qkv)
    nq = N_HEADS * HEAD_DIM
    nk = N_KV_HEADS * HEAD_DIM

    def split(lo, n_seq, seq):
        z = qkv[lo:lo + n_seq * seq]
        q = z[:, :nq].reshape(n_seq, seq, N_KV_HEADS, KV_GROUP, HEAD_DIM).transpose(0, 2, 3, 1, 4)
        kk = z[:, nq:nq + nk].reshape(n_seq, seq, N_KV_HEADS, HEAD_DIM)
        vv = z[:, nq + nk:].reshape(n_seq, seq, N_KV_HEADS, HEAD_DIM)
        return q, kk, vv

    def merge(o):
        n_seq, _, _, seq, _ = o.shape
        return o.transpose(0, 3, 1, 2, 4).reshape(n_seq * seq, nq)

    q_c, k_c, v_c = split(0, bp, sp)
    o_c = _attention(q_c, k_c.transpose(0, 2, 1, 3), v_c.transpose(0, 2, 1, 3), sinks)
    q_l, k_l, v_l = split(n_ctx, bs, ss)
    cos, sin = _rope_tables(ss)
    q_l = _rope(q_l.reshape(-1, ss, HEAD_DIM), cos, sin).reshape(q_l.shape)
    k_l = _rope(k_l.transpose(0, 2, 1, 3).reshape(-1, ss, HEAD_DIM), cos, sin)
    k_l = k_l.reshape(bs, N_KV_HEADS, ss, HEAD_DIM)
    k_all = jnp.concatenate([cache_k.transpose(0, 2, 1, 3).astype(BF16), k_l], axis=2)
    v_all = jnp.concatenate([cache_v.transpose(0, 2, 1, 3), v_l.transpose(0, 2, 1, 3)], axis=2)
    o_l = _attention(q_l, k_all, v_all, sinks, n_cache=cache_k.shape[1], window=WINDOW)
    o = jnp.concatenate([merge(o_c), merge(o_l)], axis=0)
    return _matmul(o, w_o, b_o, res=(x, mods, 2, n_ctx, ss)), k_c, v_c


def kernel(x_prompt, x_sample, c, c_ctx, state_lru_fwd, state_lru_bwd, state_rwkv_fwd, state_rwkv_bwd, cache_attn_k, cache_attn_v, norm_mix, norm_ffn, w_mod, b_mod, lru_w_in, lru_conv_w, lru_conv_b, lru_w_a, lru_b_a, lru_w_x, lru_b_x, lru_lambda, lru_w_out, pool_w, pool_b, pool_scale, rwkv_mix, rwkv_w_r, rwkv_w_k, rwkv_w_v, rwkv_w_o, rwkv_dec_w0, rwkv_dec_w1, rwkv_dec_w2, rwkv_a_w0, rwkv_a_w1, rwkv_a_w2, rwkv_g_w1, rwkv_g_w2, rwkv_k_k, rwkv_k_a, rwkv_r_k, rwkv_ln_w, rwkv_ln_b, attn_w_qkv, attn_b_qkv, attn_w_o, attn_b_o, attn_sinks, moe_w_router, moe_b_router, moe_w_gu, moe_b_gu, moe_w_down, moe_b_down, final_norm):
    bp, sp, d = x_prompt.shape
    bs, ss, _ = x_sample.shape
    depth = w_mod.shape[0]
    n_ctx = bp * sp
    assert 1 + bs <= MOD_GROUPS and sp % TOKEN_TILE == 0 and ss % 1024 == 0 and n_ctx % ss == 0
    dims = (bp, sp, bs, ss)

    x = jnp.concatenate([x_prompt.reshape(n_ctx, d), x_sample.reshape(bs * ss, d)], axis=0)
    mods_all = _all_mods(c, c_ctx, w_mod, b_mod)

    lru_f, lru_b, rwkv_f, rwkv_b, new_k, new_v = [], [], [], [], [], []
    for layer in range(depth):
        kind, j = layer % 4, layer // 4
        mods = mods_all[layer]
        if kind == 0:
            x, hf, hb = _lru_layer(x, mods, dims, norm_mix[layer], lru_w_in[j], lru_conv_w[j], lru_conv_b[j],
                                   lru_w_a[j], lru_b_a[j], lru_w_x[j], lru_b_x[j], lru_lambda[j], lru_w_out[j],
                                   state_lru_fwd[:, j:j + 1], state_lru_bwd[:, j:j + 1])
            lru_f.append(hf)
            lru_b.append(hb)
        elif kind == 1:
            x = _pool_layer(x, mods, dims, norm_mix[layer], pool_w[j], pool_b[j], pool_scale[j])
        elif kind == 2:
            x, sf, sb = _rwkv_layer(x, mods, dims, norm_mix[layer], rwkv_mix[j], rwkv_w_r[j], rwkv_w_k[j],
                                    rwkv_w_v[j], rwkv_w_o[j], rwkv_dec_w0[j], rwkv_dec_w1[j], rwkv_dec_w2[j],
                                    rwkv_a_w0[j], rwkv_a_w1[j], rwkv_a_w2[j], rwkv_g_w1[j], rwkv_g_w2[j],
                                    rwkv_k_k[j], rwkv_k_a[j], rwkv_r_k[j], rwkv_ln_w[j], rwkv_ln_b[j],
                                    state_rwkv_fwd[:, j], state_rwkv_bwd[:, j])
            rwkv_f.append(sf)
            rwkv_b.append(sb)
        else:
            x, k_c, v_c = _attn_layer(x, mods, dims, norm_mix[layer], attn_w_qkv[j], attn_b_qkv[j], attn_w_o[j],
                                      attn_b_o[j], attn_sinks[j], cache_attn_k[:, j], cache_attn_v[:, j])
            new_k.append(k_c)
            new_v.append(v_c)
        x = _moe_layer(x, norm_ffn[layer], mods, layer, moe_w_router[layer], moe_b_router[layer], moe_w_gu, moe_b_gu,
                       moe_w_down, moe_b_down, n_ctx, ss)

    y = _normmod(x, final_norm, None)
    return (y[:n_ctx].reshape(bp, sp, d), y[n_ctx:].reshape(bs, ss, d),
            jnp.concatenate(lru_f, axis=1), jnp.concatenate(lru_b, axis=1),
            jnp.stack(rwkv_f, axis=1), jnp.stack(rwkv_b, axis=1),
            jnp.stack(new_k, axis=1), jnp.stack(new_v, axis=1))
```
